```python
import jax, jax.numpy as jnp
from jax import lax
import numpy as np

D_MODEL = 1024
BATCH = 2
SEQ = 8192
DEPTH = 4

N_MIXERS = 3
EXPAND = 2
D_INNER = EXPAND * D_MODEL
CONF_KERNEL = 31
SHORT_KERNEL = 3
FOX_HEADS = 16
FOX_HEAD_DIM = D_INNER // FOX_HEADS
Q_BLOCK = 128
NORM_EPS = 1e-6
N_A = (DEPTH + 2) // 3
N_B = (DEPTH + 1) // 3
N_C = DEPTH // 3

kernel_name = "hybrid_conformer_fox_shortconv_trunk"


def rms_norm(x, g):
    xf = x.astype(jnp.float32)
    y = xf * lax.rsqrt(jnp.mean(xf * xf, axis=-1, keepdims=True) + NORM_EPS)
    return (y * g.astype(jnp.float32)).astype(x.dtype)


def layer_norm(x, g, b):
    xf = x.astype(jnp.float32)
    mu = jnp.mean(xf, axis=-1, keepdims=True)
    var = jnp.mean(jnp.square(xf - mu), axis=-1, keepdims=True)
    y = (xf - mu) * lax.rsqrt(var + NORM_EPS)
    return (y * g.astype(jnp.float32) + b.astype(jnp.float32)).astype(x.dtype)


def causal_depthwise_conv(x, w):
    k_width, channels = w.shape
    kern = w[:, None, :].astype(x.dtype)
    return lax.conv_general_dilated(
        x, kern, window_strides=(1,), padding=[(k_width - 1, 0)],
        dimension_numbers=("NWC", "WIO", "NWC"), feature_group_count=channels)


def conformer_conv_mixer(h, w_in, conv_w, conv_b, ln_g, ln_b, w_out):
    proj = h @ w_in
    val, glu_gate, z = jnp.split(proj, 3, axis=-1)
    u = val * jax.nn.sigmoid(glu_gate)
    u = causal_depthwise_conv(u, conv_w) + conv_b
    u = jax.nn.silu(layer_norm(u, ln_g, ln_b))
    return (u * jax.nn.silu(z)) @ w_out


def blocked_forgetting_attention(q, k, v, c):
    b, s_len, n_h, d_h = q.shape
    n_blocks = s_len // Q_BLOCK
    scale = d_h ** -0.5
    k_pos = jnp.arange(s_len)

    def one_block(i):
        start = i * Q_BLOCK
        q_blk = lax.dynamic_slice_in_dim(q, start, Q_BLOCK, axis=1)
        c_blk = lax.dynamic_slice_in_dim(c, start, Q_BLOCK, axis=2)
        logits = jnp.einsum("bqhd,bkhd->bhqk", q_blk, k,
                            preferred_element_type=jnp.float32) * scale
        logits = logits + c_blk[..., :, None] - c[..., None, :]
        q_pos = start + jnp.arange(Q_BLOCK)
        causal = k_pos[None, :] <= q_pos[:, None]
        logits = jnp.where(causal, logits, -jnp.inf)
        p = jax.nn.softmax(logits, axis=-1)
        return jnp.einsum("bhqk,bkhd->bqhd", p.astype(v.dtype), v)

    out = lax.map(one_block, jnp.arange(n_blocks))
    return out.transpose(1, 0, 2, 3, 4).reshape(b, s_len, n_h, d_h)


def forgetting_attention_mixer(h, w_in, f_bias, q_norm_g, k_norm_g, w_out):
    b, s_len, _ = h.shape
    proj = h @ w_in
    q, k, v, z, f_logit = jnp.split(
        proj, [D_INNER, 2 * D_INNER, 3 * D_INNER, 4 * D_INNER], axis=-1)
    q = rms_norm(q.reshape(b, s_len, FOX_HEADS, FOX_HEAD_DIM), q_norm_g)
    k = rms_norm(k.reshape(b, s_len, FOX_HEADS, FOX_HEAD_DIM), k_norm_g)
    v = v.reshape(b, s_len, FOX_HEADS, FOX_HEAD_DIM)
    log_f = jax.nn.log_sigmoid((f_logit + f_bias).astype(jnp.float32))
    c = jnp.cumsum(log_f, axis=1).transpose(0, 2, 1)
    o = blocked_forgetting_attention(q, k, v, c).reshape(b, s_len, D_INNER)
    return (o * jax.nn.silu(z)) @ w_out


def short_conv_mixer(h, w_in, conv_w, w_out):
    proj = h @ w_in
    u, b_gate, c_gate, z = jnp.split(proj, 4, axis=-1)
    y = b_gate * causal_depthwise_conv(c_gate * u, conv_w)
    return (y * jax.nn.silu(z)) @ w_out


def setup_inputs(seed: int = 0) -> dict:
    key = jax.random.key(seed)
    ks = jax.random.split(key, 20)
    f32 = jnp.float32
    nrm = lambda k, shape, s: jax.random.normal(k, shape, f32) * s
    d, e, h = D_MODEL, D_INNER, FOX_HEADS
    return {
        "x": jax.random.normal(ks[0], (BATCH, SEQ, d), f32),
        "a_norm": 1.0 + nrm(ks[1], (N_A, d), 0.05),
        "a_w_in": nrm(ks[2], (N_A, d, 3 * e), d ** -0.5),
        "a_conv_w": nrm(ks[3], (N_A, CONF_KERNEL, e), CONF_KERNEL ** -0.5),
        "a_conv_b": nrm(ks[4], (N_A, e), 0.01),
        "a_ln_g": 1.0 + nrm(ks[5], (N_A, e), 0.05),
        "a_ln_b": nrm(ks[6], (N_A, e), 0.01),
        "a_w_out": nrm(ks[7], (N_A, e, d), e ** -0.5),
        "b_norm": 1.0 + nrm(ks[8], (N_B, d), 0.05),
        "b_w_in": nrm(ks[9], (N_B, d, 4 * e + h), d ** -0.5),
        "b_f_bias": 2.0 + nrm(ks[10], (N_B, h), 0.5),
        "b_q_norm": 1.0 + nrm(ks[11], (N_B, FOX_HEAD_DIM), 0.05),
        "b_k_norm": 1.0 + nrm(ks[12], (N_B, FOX_HEAD_DIM), 0.05),
        "b_w_out": nrm(ks[13], (N_B, e, d), e ** -0.5),
        "c_norm": 1.0 + nrm(ks[14], (N_C, d), 0.05),
        "c_w_in": nrm(ks[15], (N_C, d, 4 * e), d ** -0.5),
        "c_conv_w": nrm(ks[16], (N_C, SHORT_KERNEL, e), SHORT_KERNEL ** -0.5),
        "c_w_out": nrm(ks[17], (N_C, e, d), e ** -0.5),
    }


def reference(x, a_norm, a_w_in, a_conv_w, a_conv_b, a_ln_g, a_ln_b, a_w_out,
              b_norm, b_w_in, b_f_bias, b_q_norm, b_k_norm, b_w_out,
              c_norm, c_w_in, c_conv_w, c_w_out):
    for i in range(DEPTH):
        kind, j = i % N_MIXERS, i // N_MIXERS
        if kind == 0:
            hn = rms_norm(x, a_norm[j])
            x = x + conformer_conv_mixer(hn, a_w_in[j], a_conv_w[j], a_conv_b[j],
                                         a_ln_g[j], a_ln_b[j], a_w_out[j])
        elif kind == 1:
            hn = rms_norm(x, b_norm[j])
            x = x + forgetting_attention_mixer(hn, b_w_in[j], b_f_bias[j], b_q_norm[j],
                                               b_k_norm[j], b_w_out[j])
        else:
            hn = rms_norm(x, c_norm[j])
            x = x + short_conv_mixer(hn, c_w_in[j], c_conv_w[j], c_w_out[j])
    return x
```

```python
import functools

import jax
import jax.numpy as jnp
from jax import lax
from jax.experimental import pallas as pl
from jax.experimental.pallas import tpu as pltpu

D_MODEL = 1024
D_INNER = 2048
CONF_KERNEL = 31
SHORT_KERNEL = 3
FOX_HEADS = 16
FOX_HEAD_DIM = 128
NORM_EPS = 1e-6

V7X_LANES = 128
V7X_SUBLANES = 8
V7X_VMEM_LIMIT_BYTES = 58 * 1024 * 1024

SEQ_TILE = 512
CHAN_CHUNK = 512
CONF_HALO = 32
SHORT_HALO = 8
CONV_ROWS = 64
Q_TILE = 512
KV_TILE = 512
QK_WIDTH = 2 * FOX_HEAD_DIM

BF16 = jnp.bfloat16
F32 = jnp.float32


def _sigmoid(x):
    return 1.0 / (1.0 + jnp.exp(-x))


def _silu(x):
    return x * _sigmoid(x)


def _rms_norm_rows(x, g):
    ms = jnp.mean(x * x, axis=-1, keepdims=True)
    return x * lax.rsqrt(ms + NORM_EPS) * g


def _dot(a, b):
    return jnp.dot(a, b, preferred_element_type=F32)


def _dot_nt(a, b):
    return lax.dot_general(a, b, (((1,), (1,)), ((), ())), preferred_element_type=F32)


def _split3(x):
    hi = x.astype(BF16)
    r = x - hi.astype(F32)
    mid = r.astype(BF16)
    lo = (r - mid.astype(F32)).astype(BF16)
    return hi, mid, lo


def _resident(shape):
    zeros = (0,) * len(shape)
    return pl.BlockSpec(shape, lambda *_: zeros, pipeline_mode=pl.Buffered(1))


def _params(n_axes):
    return pltpu.CompilerParams(
        dimension_semantics=("arbitrary",) * n_axes,
        vmem_limit_bytes=V7X_VMEM_LIMIT_BYTES,
    )


def _conformer_kernel(x_ref, g_ref, w_in_ref, cw_ref, cb_ref, lg_ref, lb_ref, w_out_ref,
                      o_ref, h_ref, u_ref, c_ref):
    si = pl.program_id(1)
    t = SEQ_TILE
    n_chunks = D_INNER // CHAN_CHUNK

    n_slabs = D_INNER // V7X_LANES
    slabs_per_chunk = CHAN_CHUNK // V7X_LANES

    @pl.when(si == 0)
    def _():
        u_ref[:, 0:CONF_HALO, :] = jnp.zeros((n_slabs, CONF_HALO, V7X_LANES), F32)

    x = x_ref[0]
    h_ref[...] = _rms_norm_rows(x, g_ref[...]).astype(BF16)

    for c in range(n_chunks):
        lo = c * CHAN_CHUNK
        val = _dot(h_ref[...], w_in_ref[:, lo:lo + CHAN_CHUNK])
        gate = _dot(h_ref[...], w_in_ref[:, D_INNER + lo:D_INNER + lo + CHAN_CHUNK])
        u = val * _sigmoid(gate)
        for j in range(slabs_per_chunk):
            u_ref[c * slabs_per_chunk + j, CONF_HALO:CONF_HALO + t, :] = (
                u[:, j * V7X_LANES:(j + 1) * V7X_LANES])

    first = CONF_HALO - (CONF_KERNEL - 1)

    def conv_slab(sl, carry):
        for rb in range(t // CONV_ROWS):
            r0 = rb * CONV_ROWS
            acc = jnp.zeros((CONV_ROWS, V7X_LANES), F32)
            for k in range(CONF_KERNEL):
                acc = acc + (cw_ref[sl, k:k + 1, :]
                             * u_ref[sl, r0 + first + k:r0 + first + k + CONV_ROWS, :])
            c_ref[sl, r0:r0 + CONV_ROWS, :] = acc + cb_ref[sl]
        return carry

    lax.fori_loop(0, n_slabs, conv_slab, 0)

    u_ref[:, 0:CONF_HALO, :] = u_ref[:, t:t + CONF_HALO, :]

    row_sum = c_ref[0]
    for sl in range(1, n_slabs):
        row_sum = row_sum + c_ref[sl]
    mu = jnp.sum(row_sum, axis=-1, keepdims=True) * (1.0 / D_INNER)
    sq_sum = jnp.square(c_ref[0] - mu)
    for sl in range(1, n_slabs):
        sq_sum = sq_sum + jnp.square(c_ref[sl] - mu)
    var = jnp.sum(sq_sum, axis=-1, keepdims=True) * (1.0 / D_INNER)
    rstd = lax.rsqrt(var + NORM_EPS)

    acc = x
    for c in range(n_chunks):
        lo = c * CHAN_CHUNK
        z = _dot(h_ref[...], w_in_ref[:, 2 * D_INNER + lo:2 * D_INNER + lo + CHAN_CHUNK])
        conv = jnp.concatenate(
            [c_ref[c * slabs_per_chunk + j] for j in range(slabs_per_chunk)], axis=-1)
        ln = (conv - mu) * rstd * lg_ref[:, lo:lo + CHAN_CHUNK] + lb_ref[:, lo:lo + CHAN_CHUNK]
        y = (_silu(ln) * _silu(z)).astype(BF16)
        acc = acc + _dot(y, w_out_ref[lo:lo + CHAN_CHUNK, :])
    o_ref[0] = acc


def _conformer_layer(x, g, w_in, conv_w, conv_b, ln_g, ln_b, w_out):
    b, s, d = x.shape
    t = SEQ_TILE
    row = lambda n: _resident((1, n))
    n_slabs = D_INNER // V7X_LANES
    cw = conv_w.reshape(CONF_KERNEL, n_slabs, V7X_LANES).transpose(1, 0, 2)
    cb = conv_b.reshape(n_slabs, 1, V7X_LANES)
    return pl.pallas_call(
        _conformer_kernel,
        grid=(b, s // t),
        in_specs=[
            pl.BlockSpec((1, t, d), lambda bi, si: (bi, si, 0)),
            row(d),
            _resident((d, 3 * D_INNER)),
            _resident((n_slabs, CONF_KERNEL, V7X_LANES)),
            _resident((n_slabs, 1, V7X_LANES)),
            row(D_INNER), row(D_INNER),
            _resident((D_INNER, d)),
        ],
        out_specs=pl.BlockSpec((1, t, d), lambda bi, si: (bi, si, 0)),
        out_shape=jax.ShapeDtypeStruct(x.shape, F32),
        scratch_shapes=[
            pltpu.VMEM((t, d), BF16),
            pltpu.VMEM((n_slabs, t + CONF_HALO, V7X_LANES), F32),
            pltpu.VMEM((n_slabs, t, V7X_LANES), F32),
        ],
        compiler_params=_params(2),
        name="conformer_layer",
    )(x, g.reshape(1, d), w_in.astype(BF16), cw, cb,
      ln_g.reshape(1, -1), ln_b.reshape(1, -1), w_out.astype(BF16))


def _short_conv_kernel(x_ref, g_ref, w_in_ref, cw_ref, w_out_ref, o_ref, h_ref, u_ref):
    si = pl.program_id(1)
    t = SEQ_TILE
    n_chunks = D_INNER // CHAN_CHUNK

    @pl.when(si == 0)
    def _():
        u_ref[0:SHORT_HALO, :] = jnp.zeros((SHORT_HALO, D_INNER), F32)

    x = x_ref[0]
    h_ref[...] = _rms_norm_rows(x, g_ref[...]).astype(BF16)

    acc = x
    for c in range(n_chunks):
        lo = c * CHAN_CHUNK
        cols = lambda part: slice(part * D_INNER + lo, part * D_INNER + lo + CHAN_CHUNK)
        u = _dot(h_ref[...], w_in_ref[:, cols(0)])
        c_gate = _dot(h_ref[...], w_in_ref[:, cols(2)])
        u_ref[SHORT_HALO:SHORT_HALO + t, lo:lo + CHAN_CHUNK] = c_gate * u
        conv = jnp.zeros((t, CHAN_CHUNK), F32)
        for k in range(SHORT_KERNEL):
            first = SHORT_HALO - (SHORT_KERNEL - 1) + k
            conv = conv + cw_ref[k:k + 1, lo:lo + CHAN_CHUNK] * u_ref[first:first + t, lo:lo + CHAN_CHUNK]
        b_gate = _dot(h_ref[...], w_in_ref[:, cols(1)])
        z = _dot(h_ref[...], w_in_ref[:, cols(3)])
        y = (b_gate * conv * _silu(z)).astype(BF16)
        acc = acc + _dot(y, w_out_ref[lo:lo + CHAN_CHUNK, :])
    o_ref[0] = acc

    u_ref[0:SHORT_HALO, :] = u_ref[t:t + SHORT_HALO, :]


def _short_conv_layer(x, g, w_in, conv_w, w_out):
    b, s, d = x.shape
    t = SEQ_TILE
    return pl.pallas_call(
        _short_conv_kernel,
        grid=(b, s // t),
        in_specs=[
            pl.BlockSpec((1, t, d), lambda bi, si: (bi, si, 0)),
            _resident((1, d)),
            _resident((d, 4 * D_INNER)),
            _resident((SHORT_KERNEL, D_INNER)),
            _resident((D_INNER, d)),
        ],
        out_specs=pl.BlockSpec((1, t, d), lambda bi, si: (bi, si, 0)),
        out_shape=jax.ShapeDtypeStruct(x.shape, F32),
        scratch_shapes=[
            pltpu.VMEM((t, d), BF16),
            pltpu.VMEM((t + SHORT_HALO, D_INNER), F32),
        ],
        compiler_params=_params(2),
        name="short_conv_layer",
    )(x, g.reshape(1, d), w_in.astype(BF16), conv_w, w_out.astype(BF16))


def _fox_proj_kernel(x_ref, g_ref, wqk_ref, wvt_ref, wz_ref, wf_hi_ref, wf_lo_ref, fb_ref,
                     qg_ref, kg_ref, qx_ref, kx_ref, vt_ref, gate_ref, h_ref, carry_ref):
    si = pl.program_id(1)
    t = SEQ_TILE
    dh = FOX_HEAD_DIM

    @pl.when(si == 0)
    def _():
        carry_ref[...] = jnp.zeros_like(carry_ref)

    x = x_ref[0]
    hf = _rms_norm_rows(x, g_ref[...])
    h_ref[...] = hf.astype(BF16)

    h_hi = h_ref[...]
    h_lo = (hf - h_hi.astype(F32)).astype(BF16)
    f_logit = (_dot(h_hi, wf_hi_ref[...]) + _dot(h_hi, wf_lo_ref[...])
               + _dot(h_lo, wf_hi_ref[...])) + fb_ref[...]
    log_f = jnp.minimum(f_logit, 0.0) - jnp.log1p(jnp.exp(-jnp.abs(f_logit)))
    rows = lax.broadcasted_iota(jnp.int32, (t, t), 0)
    cols = lax.broadcasted_iota(jnp.int32, (t, t), 1)
    tri = jnp.where(cols <= rows, 1.0, 0.0).astype(BF16)
    f_hi, f_mid, f_lo = _split3(log_f)
    c = (_dot(tri, f_hi) + _dot(tri, f_mid) + _dot(tri, f_lo)) + carry_ref[...]
    carry_ref[...] = c[t - 1:t, :]
    c_hi, c_mid, c_lo = _split3(c)
    c_hi, c_mid, c_lo = c_hi.astype(F32), c_mid.astype(F32), c_lo.astype(F32)

    lane = lax.broadcasted_iota(jnp.int32, (t, dh), 1)
    scale = dh ** -0.5
    heads_per_chunk = CHAN_CHUNK // dh
    for c4 in range(D_INNER // CHAN_CHUNK):
        lo = c4 * CHAN_CHUNK
        q4 = _dot(h_ref[...], wqk_ref[:, lo:lo + CHAN_CHUNK])
        k4 = _dot(h_ref[...], wqk_ref[:, D_INNER + lo:D_INNER + lo + CHAN_CHUNK])
        for j in range(heads_per_chunk):
            hd = c4 * heads_per_chunk + j
            qh = _rms_norm_rows(q4[:, j * dh:(j + 1) * dh], qg_ref[...]) * scale
            kh = _rms_norm_rows(k4[:, j * dh:(j + 1) * dh], kg_ref[...])
            hi, mid, lw = c_hi[:, hd:hd + 1], c_mid[:, hd:hd + 1], c_lo[:, hd:hd + 1]
            q_ext = jnp.where(lane < 3, 1.0,
                              jnp.where(lane == 3, hi,
                                        jnp.where(lane == 4, mid,
                                                  jnp.where(lane == 5, lw, 0.0))))
            k_ext = jnp.where(lane == 0, -hi,
                              jnp.where(lane == 1, -mid,
                                        jnp.where(lane == 2, -lw,
                                                  jnp.where(lane < 6, 1.0, 0.0))))
            qx_ref[0, hd, :, 0:dh] = qh.astype(BF16)
            qx_ref[0, hd, :, dh:2 * dh] = q_ext.astype(BF16)
            kx_ref[0, hd, :, 0:dh] = kh.astype(BF16)
            kx_ref[0, hd, :, dh:2 * dh] = k_ext.astype(BF16)
        z = _dot(h_ref[...], wz_ref[:, lo:lo + CHAN_CHUNK])
        gate_ref[0, :, lo:lo + CHAN_CHUNK] = _silu(z).astype(BF16)
        vt = _dot_nt(wvt_ref[lo:lo + CHAN_CHUNK, :], h_ref[...])
        for j in range(heads_per_chunk):
            hd = c4 * heads_per_chunk + j
            vt_ref[0, hd, 0] = vt[j * dh:(j + 1) * dh, :].astype(BF16)


def _fox_proj(x, g, w_in, f_bias, q_norm, k_norm):
    b, s, d = x.shape
    t = SEQ_TILE
    h, dh = FOX_HEADS, FOX_HEAD_DIM
    e = D_INNER
    wqk = w_in[:, :2 * e].astype(BF16)
    wvt = w_in[:, 2 * e:3 * e].T.astype(BF16)
    wz = w_in[:, 3 * e:4 * e].astype(BF16)
    wf = w_in[:, 4 * e:]
    wf_hi = wf.astype(BF16)
    wf_lo = (wf - wf_hi.astype(F32)).astype(BF16)
    return pl.pallas_call(
        _fox_proj_kernel,
        grid=(b, s // t),
        in_specs=[
            pl.BlockSpec((1, t, d), lambda bi, si: (bi, si, 0)),
            _resident((1, d)),
            _resident((d, 2 * e)),
            _resident((e, d)),
            _resident((d, e)),
            _resident((d, h)), _resident((d, h)), _resident((1, h)),
            _resident((1, dh)), _resident((1, dh)),
        ],
        out_specs=[
            pl.BlockSpec((1, h, t, QK_WIDTH), lambda bi, si: (bi, 0, si, 0)),
            pl.BlockSpec((1, h, t, QK_WIDTH), lambda bi, si: (bi, 0, si, 0)),
            pl.BlockSpec((1, h, 1, dh, t), lambda bi, si: (bi, 0, si, 0, 0)),
            pl.BlockSpec((1, t, e), lambda bi, si: (bi, si, 0)),
        ],
        out_shape=[
            jax.ShapeDtypeStruct((b, h, s, QK_WIDTH), BF16),
            jax.ShapeDtypeStruct((b, h, s, QK_WIDTH), BF16),
            jax.ShapeDtypeStruct((b, h, s // t, dh, t), BF16),
            jax.ShapeDtypeStruct((b, s, e), BF16),
        ],
        scratch_shapes=[
            pltpu.VMEM((t, d), BF16),
            pltpu.VMEM((1, h), F32),
        ],
        compiler_params=_params(2),
        name="fox_proj",
    )(x, g.reshape(1, d), wqk, wvt, wz, wf_hi, wf_lo, f_bias.reshape(1, h),
      q_norm.reshape(1, dh), k_norm.reshape(1, dh))


def _fox_attn_kernel(qx_ref, kx_ref, vt_ref, gate_ref, y_ref, m_ref, l_ref, acc_ref):
    qi = pl.program_id(2)
    q = qx_ref[0, 0]

    m_ref[...] = jnp.full(m_ref.shape, -1e30, F32)
    l_ref[...] = jnp.zeros(l_ref.shape, F32)
    acc_ref[...] = jnp.zeros(acc_ref.shape, F32)

    def block(kj, masked):
        k0 = pl.multiple_of(kj * KV_TILE, KV_TILE)
        s = _dot_nt(kx_ref[0, 0, pl.ds(k0, KV_TILE), :], q)
        if masked:
            kv_pos = lax.broadcasted_iota(jnp.int32, s.shape, 0)
            q_pos = lax.broadcasted_iota(jnp.int32, s.shape, 1)
            s = jnp.where(kv_pos <= q_pos, s, -jnp.inf)
        m_prev = m_ref[...]
        m_new = jnp.maximum(m_prev, jnp.max(s, axis=0, keepdims=True))
        p = jnp.exp(s - m_new)
        alpha = jnp.exp(m_prev - m_new)
        l_ref[...] = alpha * l_ref[...] + jnp.sum(p, axis=0, keepdims=True)
        acc_ref[...] = alpha * acc_ref[...] + _dot(vt_ref[0, 0, kj], p.astype(BF16))
        m_ref[...] = m_new

    def body(kj, carry):
        block(kj, masked=False)
        return carry

    lax.fori_loop(0, qi, body, 0)
    block(qi, masked=True)

    o_t = acc_ref[...] * (1.0 / l_ref[...])
    y_ref[0] = (o_t.T * gate_ref[0].astype(F32)).astype(BF16)


def _fox_attention(qx, kx, vt, gate):
    b, h, s, _ = qx.shape
    dh = FOX_HEAD_DIM
    assert Q_TILE == KV_TILE == SEQ_TILE
    return pl.pallas_call(
        _fox_attn_kernel,
        grid=(b, h, s // Q_TILE),
        in_specs=[
            pl.BlockSpec((1, 1, Q_TILE, QK_WIDTH), lambda bi, hi, qi: (bi, hi, qi, 0)),
            pl.BlockSpec((1, 1, s, QK_WIDTH), lambda bi, hi, qi: (bi, hi, 0, 0)),
            pl.BlockSpec((1, 1, s // KV_TILE, dh, KV_TILE), lambda bi, hi, qi: (bi, hi, 0, 0, 0)),
            pl.BlockSpec((1, Q_TILE, dh), lambda bi, hi, qi: (bi, qi, hi)),
        ],
        out_specs=pl.BlockSpec((1, Q_TILE, dh), lambda bi, hi, qi: (bi, qi, hi)),
        out_shape=jax.ShapeDtypeStruct((b, s, h * dh), BF16),
        scratch_shapes=[
            pltpu.VMEM((1, Q_TILE), F32),
            pltpu.VMEM((1, Q_TILE), F32),
            pltpu.VMEM((dh, Q_TILE), F32),
        ],
        compiler_params=_params(3),
        name="fox_attention",
    )(qx, kx, vt, gate)


def _out_proj_kernel(x_ref, y_ref, w_ref, o_ref):
    o_ref[0] = x_ref[0] + _dot(y_ref[0], w_ref[...])


def _out_proj(x, y, w_out):
    b, s, d = x.shape
    t = SEQ_TILE
    e = y.shape[-1]
    return pl.pallas_call(
        _out_proj_kernel,
        grid=(b, s // t),
        in_specs=[
            pl.BlockSpec((1, t, d), lambda bi, si: (bi, si, 0)),
            pl.BlockSpec((1, t, e), lambda bi, si: (bi, si, 0)),
            _resident((e, d)),
        ],
        out_specs=pl.BlockSpec((1, t, d), lambda bi, si: (bi, si, 0)),
        out_shape=jax.ShapeDtypeStruct(x.shape, F32),
        compiler_params=_params(2),
        name="fox_out_proj",
    )(x, y, w_out.astype(BF16))


def _fox_layer(x, g, w_in, f_bias, q_norm, k_norm, w_out):
    qx, kx, vt, gate = _fox_proj(x, g, w_in, f_bias, q_norm, k_norm)
    y = _fox_attention(qx, kx, vt, gate)
    return _out_proj(x, y, w_out)


def kernel(x, a_norm, a_w_in, a_conv_w, a_conv_b, a_ln_g, a_ln_b, a_w_out, b_norm, b_w_in, b_f_bias, b_q_norm, b_k_norm, b_w_out, c_norm, c_w_in, c_conv_w, c_w_out):
    depth = a_norm.shape[0] + b_norm.shape[0] + c_norm.shape[0]
    for i in range(depth):
        kind, j = i % 3, i // 3
        if kind == 0:
            x = _conformer_layer(x, a_norm[j], a_w_in[j], a_conv_w[j], a_conv_b[j],
                                 a_ln_g[j], a_ln_b[j], a_w_out[j])
        elif kind == 1:
            x = _fox_layer(x, b_norm[j], b_w_in[j], b_f_bias[j], b_q_norm[j], b_k_norm[j],
                           b_w_out[j])
        else:
            x = _short_conv_layer(x, c_norm[j], c_w_in[j], c_conv_w[j], c_w_out[j])
    return x
```

```python
import functools

import jax
import jax.numpy as jnp
from jax import lax
from jax.experimental import pallas as pl
from jax.experimental.pallas import tpu as pltpu

D_MODEL = 1024
D_INNER = 2048
CONF_KERNEL = 31
SHORT_KERNEL = 3
FOX_HEADS = 16
FOX_HEAD_DIM = 128
NORM_EPS = 1e-6

V7X_LANES = 128
V7X_SUBLANES = 8
V7X_VMEM_LIMIT_BYTES = 58 * 1024 * 1024

SEQ_TILE = 512
CHAN_CHUNK = 512
CONF_HALO = 32
SHORT_HALO = 8
CONV_ROWS = 64
KV_TILE = 512
Q_TILE = 2 * KV_TILE
QK_WIDTH = 2 * FOX_HEAD_DIM
LOG2_E = 1.4426950408889634

BF16 = jnp.bfloat16
F32 = jnp.float32


def _sigmoid(x):
    return 1.0 / (1.0 + jnp.exp(-x))


def _silu(x):
    return x * _sigmoid(x)


def _rms_norm_rows(x, g):
    ms = jnp.mean(x * x, axis=-1, keepdims=True)
    return x * lax.rsqrt(ms + NORM_EPS) * g


def _dot(a, b):
    return jnp.dot(a, b, preferred_element_type=F32)


def _dot_nt(a, b):
    return lax.dot_general(a, b, (((1,), (1,)), ((), ())), preferred_element_type=F32)


def _split3(x):
    hi = x.astype(BF16)
    r = x - hi.astype(F32)
    mid = r.astype(BF16)
    lo = (r - mid.astype(F32)).astype(BF16)
    return hi, mid, lo


def _resident(shape):
    zeros = (0,) * len(shape)
    return pl.BlockSpec(shape, lambda *_: zeros, pipeline_mode=pl.Buffered(1))


def _params(n_axes):
    return pltpu.CompilerParams(
        dimension_semantics=("arbitrary",) * n_axes,
        vmem_limit_bytes=V7X_VMEM_LIMIT_BYTES,
    )


def _conformer_kernel(x_ref, g_ref, w_in_ref, cw_ref, cb_ref, lg_ref, lb_ref, w_out_ref,
                      o_ref, h_ref, u_ref, c_ref):
    si = pl.program_id(1)
    t = SEQ_TILE
    n_chunks = D_INNER // CHAN_CHUNK

    n_slabs = D_INNER // V7X_LANES
    slabs_per_chunk = CHAN_CHUNK // V7X_LANES

    @pl.when(si == 0)
    def _():
        u_ref[:, 0:CONF_HALO, :] = jnp.zeros((n_slabs, CONF_HALO, V7X_LANES), F32)

    x = x_ref[0]
    h_ref[...] = _rms_norm_rows(x, g_ref[...]).astype(BF16)

    for c in range(n_chunks):
        lo = c * CHAN_CHUNK
        val = _dot(h_ref[...], w_in_ref[:, lo:lo + CHAN_CHUNK])
        gate = _dot(h_ref[...], w_in_ref[:, D_INNER + lo:D_INNER + lo + CHAN_CHUNK])
        u = val * _sigmoid(gate)
        for j in range(slabs_per_chunk):
            u_ref[c * slabs_per_chunk + j, CONF_HALO:CONF_HALO + t, :] = (
                u[:, j * V7X_LANES:(j + 1) * V7X_LANES])

    first = CONF_HALO - (CONF_KERNEL - 1)

    def conv_slab(sl, carry):
        for rb in range(t // CONV_ROWS):
            r0 = rb * CONV_ROWS
            acc = jnp.zeros((CONV_ROWS, V7X_LANES), F32)
            for k in range(CONF_KERNEL):
                acc = acc + (cw_ref[sl, k:k + 1, :]
                             * u_ref[sl, r0 + first + k:r0 + first + k + CONV_ROWS, :])
            c_ref[sl, r0:r0 + CONV_ROWS, :] = acc + cb_ref[sl]
        return carry

    lax.fori_loop(0, n_slabs, conv_slab, 0)

    u_ref[:, 0:CONF_HALO, :] = u_ref[:, t:t + CONF_HALO, :]

    row_sum = c_ref[0]
    for sl in range(1, n_slabs):
        row_sum = row_sum + c_ref[sl]
    mu = jnp.sum(row_sum, axis=-1, keepdims=True) * (1.0 / D_INNER)
    sq_sum = jnp.square(c_ref[0] - mu)
    for sl in range(1, n_slabs):
        sq_sum = sq_sum + jnp.square(c_ref[sl] - mu)
    var = jnp.sum(sq_sum, axis=-1, keepdims=True) * (1.0 / D_INNER)
    rstd = lax.rsqrt(var + NORM_EPS)

    acc = x
    for c in range(n_chunks):
        lo = c * CHAN_CHUNK
        z = _dot(h_ref[...], w_in_ref[:, 2 * D_INNER + lo:2 * D_INNER + lo + CHAN_CHUNK])
        conv = jnp.concatenate(
            [c_ref[c * slabs_per_chunk + j] for j in range(slabs_per_chunk)], axis=-1)
        ln = (conv - mu) * rstd * lg_ref[:, lo:lo + CHAN_CHUNK] + lb_ref[:, lo:lo + CHAN_CHUNK]
        y = (_silu(ln) * _silu(z)).astype(BF16)
        acc = acc + _dot(y, w_out_ref[lo:lo + CHAN_CHUNK, :])
    o_ref[0] = acc


def _conformer_layer(x, g, w_in, conv_w, conv_b, ln_g, ln_b, w_out):
    b, s, d = x.shape
    t = SEQ_TILE
    row = lambda n: _resident((1, n))
    n_slabs = D_INNER // V7X_LANES
    cw = conv_w.reshape(CONF_KERNEL, n_slabs, V7X_LANES).transpose(1, 0, 2)
    cb = conv_b.reshape(n_slabs, 1, V7X_LANES)
    return pl.pallas_call(
        _conformer_kernel,
        grid=(b, s // t),
        in_specs=[
            pl.BlockSpec((1, t, d), lambda bi, si: (bi, si, 0)),
            row(d),
            _resident((d, 3 * D_INNER)),
            _resident((n_slabs, CONF_KERNEL, V7X_LANES)),
            _resident((n_slabs, 1, V7X_LANES)),
            row(D_INNER), row(D_INNER),
            _resident((D_INNER, d)),
        ],
        out_specs=pl.BlockSpec((1, t, d), lambda bi, si: (bi, si, 0)),
        out_shape=jax.ShapeDtypeStruct(x.shape, F32),
        scratch_shapes=[
            pltpu.VMEM((t, d), BF16),
            pltpu.VMEM((n_slabs, t + CONF_HALO, V7X_LANES), F32),
            pltpu.VMEM((n_slabs, t, V7X_LANES), F32),
        ],
        compiler_params=_params(2),
        name="conformer_layer",
    )(x, g.reshape(1, d), w_in.astype(BF16), cw, cb,
      ln_g.reshape(1, -1), ln_b.reshape(1, -1), w_out.astype(BF16))


def _short_conv_kernel(x_ref, g_ref, w_in_ref, cw_ref, w_out_ref, o_ref, h_ref, u_ref):
    si = pl.program_id(1)
    t = SEQ_TILE
    n_chunks = D_INNER // CHAN_CHUNK

    @pl.when(si == 0)
    def _():
        u_ref[0:SHORT_HALO, :] = jnp.zeros((SHORT_HALO, D_INNER), F32)

    x = x_ref[0]
    h_ref[...] = _rms_norm_rows(x, g_ref[...]).astype(BF16)

    acc = x
    for c in range(n_chunks):
        lo = c * CHAN_CHUNK
        cols = lambda part: slice(part * D_INNER + lo, part * D_INNER + lo + CHAN_CHUNK)
        u = _dot(h_ref[...], w_in_ref[:, cols(0)])
        c_gate = _dot(h_ref[...], w_in_ref[:, cols(2)])
        u_ref[SHORT_HALO:SHORT_HALO + t, lo:lo + CHAN_CHUNK] = c_gate * u
        conv = jnp.zeros((t, CHAN_CHUNK), F32)
        for k in range(SHORT_KERNEL):
            first = SHORT_HALO - (SHORT_KERNEL - 1) + k
            conv = conv + cw_ref[k:k + 1, lo:lo + CHAN_CHUNK] * u_ref[first:first + t, lo:lo + CHAN_CHUNK]
        b_gate = _dot(h_ref[...], w_in_ref[:, cols(1)])
        z = _dot(h_ref[...], w_in_ref[:, cols(3)])
        y = (b_gate * conv * _silu(z)).astype(BF16)
        acc = acc + _dot(y, w_out_ref[lo:lo + CHAN_CHUNK, :])
    o_ref[0] = acc

    u_ref[0:SHORT_HALO, :] = u_ref[t:t + SHORT_HALO, :]


def _short_conv_layer(x, g, w_in, conv_w, w_out):
    b, s, d = x.shape
    t = SEQ_TILE
    return pl.pallas_call(
        _short_conv_kernel,
        grid=(b, s // t),
        in_specs=[
            pl.BlockSpec((1, t, d), lambda bi, si: (bi, si, 0)),
            _resident((1, d)),
            _resident((d, 4 * D_INNER)),
            _resident((SHORT_KERNEL, D_INNER)),
            _resident((D_INNER, d)),
        ],
        out_specs=pl.BlockSpec((1, t, d), lambda bi, si: (bi, si, 0)),
        out_shape=jax.ShapeDtypeStruct(x.shape, F32),
        scratch_shapes=[
            pltpu.VMEM((t, d), BF16),
            pltpu.VMEM((t + SHORT_HALO, D_INNER), F32),
        ],
        compiler_params=_params(2),
        name="short_conv_layer",
    )(x, g.reshape(1, d), w_in.astype(BF16), conv_w, w_out.astype(BF16))


def _fox_proj_kernel(x_ref, g_ref, wqk_ref, wvt_ref, wz_ref, wf_hi_ref, wf_lo_ref, fb_ref,
                     qg_ref, kg_ref, qx_ref, kx_ref, vt_ref, gate_ref, h_ref, carry_ref):
    si = pl.program_id(1)
    t = SEQ_TILE
    dh = FOX_HEAD_DIM

    @pl.when(si == 0)
    def _():
        carry_ref[...] = jnp.zeros_like(carry_ref)

    x = x_ref[0]
    hf = _rms_norm_rows(x, g_ref[...])
    h_ref[...] = hf.astype(BF16)

    h_hi = h_ref[...]
    h_lo = (hf - h_hi.astype(F32)).astype(BF16)
    f_logit = (_dot(h_hi, wf_hi_ref[...]) + _dot(h_hi, wf_lo_ref[...])
               + _dot(h_lo, wf_hi_ref[...])) + fb_ref[...]
    log_f = jnp.minimum(f_logit, 0.0) - jnp.log1p(jnp.exp(-jnp.abs(f_logit)))
    rows = lax.broadcasted_iota(jnp.int32, (t, t), 0)
    cols = lax.broadcasted_iota(jnp.int32, (t, t), 1)
    tri = jnp.where(cols <= rows, 1.0, 0.0).astype(BF16)
    f_hi, f_mid, f_lo = _split3(log_f)
    c = (_dot(tri, f_hi) + _dot(tri, f_mid) + _dot(tri, f_lo)) + carry_ref[...]
    carry_ref[...] = c[t - 1:t, :]
    c_hi, c_mid, c_lo = _split3(c * LOG2_E)
    c_hi, c_mid, c_lo = c_hi.astype(F32), c_mid.astype(F32), c_lo.astype(F32)

    lane = lax.broadcasted_iota(jnp.int32, (t, dh), 1)
    scale = dh ** -0.5 * LOG2_E
    heads_per_chunk = CHAN_CHUNK // dh
    for c4 in range(D_INNER // CHAN_CHUNK):
        lo = c4 * CHAN_CHUNK
        q4 = _dot(h_ref[...], wqk_ref[:, lo:lo + CHAN_CHUNK])
        k4 = _dot(h_ref[...], wqk_ref[:, D_INNER + lo:D_INNER + lo + CHAN_CHUNK])
        for j in range(heads_per_chunk):
            hd = c4 * heads_per_chunk + j
            qh = _rms_norm_rows(q4[:, j * dh:(j + 1) * dh], qg_ref[...]) * scale
            kh = _rms_norm_rows(k4[:, j * dh:(j + 1) * dh], kg_ref[...])
            hi, mid, lw = c_hi[:, hd:hd + 1], c_mid[:, hd:hd + 1], c_lo[:, hd:hd + 1]
            q_ext = jnp.where(lane < 3, 1.0,
                              jnp.where(lane == 3, hi,
                                        jnp.where(lane == 4, mid,
                                                  jnp.where(lane == 5, lw, 0.0))))
            k_ext = jnp.where(lane == 0, -hi,
                              jnp.where(lane == 1, -mid,
                                        jnp.where(lane == 2, -lw,
                                                  jnp.where(lane < 6, 1.0, 0.0))))
            qx_ref[0, hd, :, 0:dh] = qh.astype(BF16)
            qx_ref[0, hd, :, dh:2 * dh] = q_ext.astype(BF16)
            kx_ref[0, hd, :, 0:dh] = kh.astype(BF16)
            kx_ref[0, hd, :, dh:2 * dh] = k_ext.astype(BF16)
        z = _dot(h_ref[...], wz_ref[:, lo:lo + CHAN_CHUNK])
        gate_ref[0, :, lo:lo + CHAN_CHUNK] = _silu(z).astype(BF16)
        vt = _dot_nt(wvt_ref[lo:lo + CHAN_CHUNK, :], h_ref[...])
        for j in range(heads_per_chunk):
            hd = c4 * heads_per_chunk + j
            vt_ref[0, hd, 0] = vt[j * dh:(j + 1) * dh, :].astype(BF16)


def _fox_proj(x, g, w_in, f_bias, q_norm, k_norm):
    b, s, d = x.shape
    t = SEQ_TILE
    h, dh = FOX_HEADS, FOX_HEAD_DIM
    e = D_INNER
    wqk = w_in[:, :2 * e].astype(BF16)
    wvt = w_in[:, 2 * e:3 * e].T.astype(BF16)
    wz = w_in[:, 3 * e:4 * e].astype(BF16)
    wf = w_in[:, 4 * e:]
    wf_hi = wf.astype(BF16)
    wf_lo = (wf - wf_hi.astype(F32)).astype(BF16)
    return pl.pallas_call(
        _fox_proj_kernel,
        grid=(b, s // t),
        in_specs=[
            pl.BlockSpec((1, t, d), lambda bi, si: (bi, si, 0)),
            _resident((1, d)),
            _resident((d, 2 * e)),
            _resident((e, d)),
            _resident((d, e)),
            _resident((d, h)), _resident((d, h)), _resident((1, h)),
            _resident((1, dh)), _resident((1, dh)),
        ],
        out_specs=[
            pl.BlockSpec((1, h, t, QK_WIDTH), lambda bi, si: (bi, 0, si, 0)),
            pl.BlockSpec((1, h, t, QK_WIDTH), lambda bi, si: (bi, 0, si, 0)),
            pl.BlockSpec((1, h, 1, dh, t), lambda bi, si: (bi, 0, si, 0, 0)),
            pl.BlockSpec((1, t, e), lambda bi, si: (bi, si, 0)),
        ],
        out_shape=[
            jax.ShapeDtypeStruct((b, h, s, QK_WIDTH), BF16),
            jax.ShapeDtypeStruct((b, h, s, QK_WIDTH), BF16),
            jax.ShapeDtypeStruct((b, h, s // t, dh, t), BF16),
            jax.ShapeDtypeStruct((b, s, e), BF16),
        ],
        scratch_shapes=[
            pltpu.VMEM((t, d), BF16),
            pltpu.VMEM((1, h), F32),
        ],
        compiler_params=_params(2),
        name="fox_proj",
    )(x, g.reshape(1, d), wqk, wvt, wz, wf_hi, wf_lo, f_bias.reshape(1, h),
      q_norm.reshape(1, dh), k_norm.reshape(1, dh))


def _fox_attn_kernel(qx_ref, kx_ref, vt_ref, gate_ref, y_ref,
                     s0_ref, s1_ref, p0_ref, p1_ref, a0_ref, a1_ref, m_ref, l_ref, acc_ref):
    qi = pl.program_id(2)
    kt = KV_TILE

    def scores(blk, s_ref, c0, c1):
        k0 = pl.multiple_of(blk * kt, kt)
        s_ref[:, c0:c1] = _dot_nt(kx_ref[0, 0, pl.ds(k0, kt), :], qx_ref[0, 0, c0:c1, :])

    def softmax(s_ref, p_ref, a_ref, c0, c1, masked):
        s = s_ref[:, c0:c1]
        if masked:
            kv_pos = lax.broadcasted_iota(jnp.int32, s.shape, 0)
            q_pos = lax.broadcasted_iota(jnp.int32, s.shape, 1)
            s = jnp.where(kv_pos <= q_pos, s, -jnp.inf)
        m_prev = m_ref[:, c0:c1]
        m_new = jnp.maximum(m_prev, jnp.max(s, axis=0, keepdims=True))
        p = jnp.exp2(s - m_new)
        alpha = jnp.exp2(m_prev - m_new)
        l_ref[:, c0:c1] = alpha * l_ref[:, c0:c1] + jnp.sum(p, axis=0, keepdims=True)
        m_ref[:, c0:c1] = m_new
        a_ref[:, c0:c1] = alpha
        p_ref[:, c0:c1] = p.astype(BF16)

    def accumulate(blk, p_ref, a_ref, c0, c1):
        acc_ref[:, c0:c1] = (a_ref[:, c0:c1] * acc_ref[:, c0:c1]
                             + _dot(vt_ref[0, 0, blk], p_ref[:, c0:c1]))

    m_ref[...] = jnp.full(m_ref.shape, -1e30, F32)
    l_ref[...] = jnp.zeros(l_ref.shape, F32)
    acc_ref[...] = jnp.zeros(acc_ref.shape, F32)
    p1_ref[...] = jnp.zeros(p1_ref.shape, BF16)
    a1_ref[...] = jnp.ones(a1_ref.shape, F32)

    full = (0, Q_TILE)
    scores(0, s0_ref, *full)

    def pair(t, carry):
        even = 2 * t
        scores(even + 1, s1_ref, *full)
        softmax(s0_ref, p0_ref, a0_ref, *full, masked=False)
        accumulate(jnp.maximum(even - 1, 0), p1_ref, a1_ref, *full)
        scores(even + 2, s0_ref, *full)
        softmax(s1_ref, p1_ref, a1_ref, *full, masked=False)
        accumulate(even, p0_ref, a0_ref, *full)
        return carry

    lax.fori_loop(0, qi, pair, 0)

    d0 = 2 * qi
    left, right = (0, kt), (kt, Q_TILE)
    scores(d0 + 1, s1_ref, *right)
    softmax(s0_ref, p0_ref, a0_ref, *left, masked=True)
    softmax(s0_ref, p0_ref, a0_ref, *right, masked=False)
    accumulate(jnp.maximum(d0 - 1, 0), p1_ref, a1_ref, *full)
    softmax(s1_ref, p1_ref, a1_ref, *right, masked=True)
    accumulate(d0, p0_ref, a0_ref, *full)
    accumulate(d0 + 1, p1_ref, a1_ref, *right)

    o_t = acc_ref[...] * (1.0 / l_ref[...])
    y_ref[0] = (o_t.T * gate_ref[0].astype(F32)).astype(BF16)


def _fox_attention(qx, kx, vt, gate):
    b, h, s, _ = qx.shape
    dh = FOX_HEAD_DIM
    assert KV_TILE == SEQ_TILE and Q_TILE == 2 * KV_TILE
    return pl.pallas_call(
        _fox_attn_kernel,
        grid=(b, h, s // Q_TILE),
        in_specs=[
            pl.BlockSpec((1, 1, Q_TILE, QK_WIDTH), lambda bi, hi, qi: (bi, hi, qi, 0)),
            pl.BlockSpec((1, 1, s, QK_WIDTH), lambda bi, hi, qi: (bi, hi, 0, 0)),
            pl.BlockSpec((1, 1, s // KV_TILE, dh, KV_TILE), lambda bi, hi, qi: (bi, hi, 0, 0, 0)),
            pl.BlockSpec((1, Q_TILE, dh), lambda bi, hi, qi: (bi, qi, hi)),
        ],
        out_specs=pl.BlockSpec((1, Q_TILE, dh), lambda bi, hi, qi: (bi, qi, hi)),
        out_shape=jax.ShapeDtypeStruct((b, s, h * dh), BF16),
        scratch_shapes=[
            pltpu.VMEM((KV_TILE, Q_TILE), F32),
            pltpu.VMEM((KV_TILE, Q_TILE), F32),
            pltpu.VMEM((KV_TILE, Q_TILE), BF16),
            pltpu.VMEM((KV_TILE, Q_TILE), BF16),
            pltpu.VMEM((1, Q_TILE), F32),
            pltpu.VMEM((1, Q_TILE), F32),
            pltpu.VMEM((1, Q_TILE), F32),
            pltpu.VMEM((1, Q_TILE), F32),
            pltpu.VMEM((dh, Q_TILE), F32),
        ],
        compiler_params=_params(3),
        name="fox_attention",
    )(qx, kx, vt, gate)


def _out_proj_kernel(x_ref, y_ref, w_ref, o_ref):
    o_ref[0] = x_ref[0] + _dot(y_ref[0], w_ref[...])


def _out_proj(x, y, w_out):
    b, s, d = x.shape
    t = SEQ_TILE
    e = y.shape[-1]
    return pl.pallas_call(
        _out_proj_kernel,
        grid=(b, s // t),
        in_specs=[
            pl.BlockSpec((1, t, d), lambda bi, si: (bi, si, 0)),
            pl.BlockSpec((1, t, e), lambda bi, si: (bi, si, 0)),
            _resident((e, d)),
        ],
        out_specs=pl.BlockSpec((1, t, d), lambda bi, si: (bi, si, 0)),
        out_shape=jax.ShapeDtypeStruct(x.shape, F32),
        compiler_params=_params(2),
        name="fox_out_proj",
    )(x, y, w_out.astype(BF16))


def _fox_layer(x, g, w_in, f_bias, q_norm, k_norm, w_out):
    qx, kx, vt, gate = _fox_proj(x, g, w_in, f_bias, q_norm, k_norm)
    y = _fox_attention(qx, kx, vt, gate)
    return _out_proj(x, y, w_out)


def kernel(x, a_norm, a_w_in, a_conv_w, a_conv_b, a_ln_g, a_ln_b, a_w_out, b_norm, b_w_in, b_f_bias, b_q_norm, b_k_norm, b_w_out, c_norm, c_w_in, c_conv_w, c_w_out):
    depth = a_norm.shape[0] + b_norm.shape[0] + c_norm.shape[0]
    for i in range(depth):
        kind, j = i % 3, i // 3
        if kind == 0:
            x = _conformer_layer(x, a_norm[j], a_w_in[j], a_conv_w[j], a_conv_b[j],
                                 a_ln_g[j], a_ln_b[j], a_w_out[j])
        elif kind == 1:
            x = _fox_layer(x, b_norm[j], b_w_in[j], b_f_bias[j], b_q_norm[j], b_k_norm[j],
                           b_w_out[j])
        else:
            x = _short_conv_layer(x, c_norm[j], c_w_in[j], c_conv_w[j], c_w_out[j])
    return x
```

```python
import functools

import jax
import jax.numpy as jnp
from jax import lax
from jax.experimental import pallas as pl
from jax.experimental.pallas import tpu as pltpu

D_MODEL = 1024
D_INNER = 2048
CONF_KERNEL = 31
SHORT_KERNEL = 3
FOX_HEADS = 16
FOX_HEAD_DIM = 128
NORM_EPS = 1e-6

V7X_LANES = 128
V7X_SUBLANES = 8
V7X_VMEM_LIMIT_BYTES = 58 * 1024 * 1024

SEQ_TILE = 512
CHAN_CHUNK = 512
CONF_HALO = 32
SHORT_HALO = 8
CONV_ROWS = 64
KV_TILE = 512
Q_TILE = 2 * KV_TILE
QK_WIDTH = 2 * FOX_HEAD_DIM
LOG2_E = 1.4426950408889634
BOUNDED_LOGIT_LOG2 = 50.0

BF16 = jnp.bfloat16
F32 = jnp.float32


def _sigmoid(x):
    return 1.0 / (1.0 + jnp.exp(-x))


def _silu(x):
    return x * _sigmoid(x)


def _rms_norm_rows(x, g):
    ms = jnp.mean(x * x, axis=-1, keepdims=True)
    return x * lax.rsqrt(ms + NORM_EPS) * g


def _dot(a, b):
    return jnp.dot(a, b, preferred_element_type=F32)


def _dot_nt(a, b):
    return lax.dot_general(a, b, (((1,), (1,)), ((), ())), preferred_element_type=F32)


def _split3(x):
    hi = x.astype(BF16)
    r = x - hi.astype(F32)
    mid = r.astype(BF16)
    lo = (r - mid.astype(F32)).astype(BF16)
    return hi, mid, lo


def _resident(shape):
    zeros = (0,) * len(shape)
    return pl.BlockSpec(shape, lambda *_: zeros, pipeline_mode=pl.Buffered(1))


def _params(n_axes):
    return pltpu.CompilerParams(
        dimension_semantics=("arbitrary",) * n_axes,
        vmem_limit_bytes=V7X_VMEM_LIMIT_BYTES,
    )


def _conformer_kernel(x_ref, g_ref, w_in_ref, cw_ref, cb_ref, lg_ref, lb_ref, w_out_ref,
                      o_ref, h_ref, u_ref, c_ref):
    si = pl.program_id(1)
    t = SEQ_TILE
    n_chunks = D_INNER // CHAN_CHUNK

    n_slabs = D_INNER // V7X_LANES
    slabs_per_chunk = CHAN_CHUNK // V7X_LANES

    @pl.when(si == 0)
    def _():
        u_ref[:, 0:CONF_HALO, :] = jnp.zeros((n_slabs, CONF_HALO, V7X_LANES), F32)

    x = x_ref[0]
    h_ref[...] = _rms_norm_rows(x, g_ref[...]).astype(BF16)

    for c in range(n_chunks):
        lo = c * CHAN_CHUNK
        val = _dot(h_ref[...], w_in_ref[:, lo:lo + CHAN_CHUNK])
        gate = _dot(h_ref[...], w_in_ref[:, D_INNER + lo:D_INNER + lo + CHAN_CHUNK])
        u = val * _sigmoid(gate)
        for j in range(slabs_per_chunk):
            u_ref[c * slabs_per_chunk + j, CONF_HALO:CONF_HALO + t, :] = (
                u[:, j * V7X_LANES:(j + 1) * V7X_LANES])

    first = CONF_HALO - (CONF_KERNEL - 1)

    def conv_slab(sl, carry):
        for rb in range(t // CONV_ROWS):
            r0 = rb * CONV_ROWS
            acc = jnp.zeros((CONV_ROWS, V7X_LANES), F32)
            for k in range(CONF_KERNEL):
                acc = acc + (cw_ref[sl, k:k + 1, :]
                             * u_ref[sl, r0 + first + k:r0 + first + k + CONV_ROWS, :])
            c_ref[sl, r0:r0 + CONV_ROWS, :] = acc + cb_ref[sl]
        return carry

    lax.fori_loop(0, n_slabs, conv_slab, 0)

    u_ref[:, 0:CONF_HALO, :] = u_ref[:, t:t + CONF_HALO, :]

    row_sum = c_ref[0]
    for sl in range(1, n_slabs):
        row_sum = row_sum + c_ref[sl]
    mu = jnp.sum(row_sum, axis=-1, keepdims=True) * (1.0 / D_INNER)
    sq_sum = jnp.square(c_ref[0] - mu)
    for sl in range(1, n_slabs):
        sq_sum = sq_sum + jnp.square(c_ref[sl] - mu)
    var = jnp.sum(sq_sum, axis=-1, keepdims=True) * (1.0 / D_INNER)
    rstd = lax.rsqrt(var + NORM_EPS)

    acc = x
    for c in range(n_chunks):
        lo = c * CHAN_CHUNK
        z = _dot(h_ref[...], w_in_ref[:, 2 * D_INNER + lo:2 * D_INNER + lo + CHAN_CHUNK])
        conv = jnp.concatenate(
            [c_ref[c * slabs_per_chunk + j] for j in range(slabs_per_chunk)], axis=-1)
        ln = (conv - mu) * rstd * lg_ref[:, lo:lo + CHAN_CHUNK] + lb_ref[:, lo:lo + CHAN_CHUNK]
        y = (_silu(ln) * _silu(z)).astype(BF16)
        acc = acc + _dot(y, w_out_ref[lo:lo + CHAN_CHUNK, :])
    o_ref[0] = acc


def _conformer_layer(x, g, w_in, conv_w, conv_b, ln_g, ln_b, w_out):
    b, s, d = x.shape
    t = SEQ_TILE
    row = lambda n: _resident((1, n))
    n_slabs = D_INNER // V7X_LANES
    cw = conv_w.reshape(CONF_KERNEL, n_slabs, V7X_LANES).transpose(1, 0, 2)
    cb = conv_b.reshape(n_slabs, 1, V7X_LANES)
    return pl.pallas_call(
        _conformer_kernel,
        grid=(b, s // t),
        in_specs=[
            pl.BlockSpec((1, t, d), lambda bi, si: (bi, si, 0)),
            row(d),
            _resident((d, 3 * D_INNER)),
            _resident((n_slabs, CONF_KERNEL, V7X_LANES)),
            _resident((n_slabs, 1, V7X_LANES)),
            row(D_INNER), row(D_INNER),
            _resident((D_INNER, d)),
        ],
        out_specs=pl.BlockSpec((1, t, d), lambda bi, si: (bi, si, 0)),
        out_shape=jax.ShapeDtypeStruct(x.shape, F32),
        scratch_shapes=[
            pltpu.VMEM((t, d), BF16),
            pltpu.VMEM((n_slabs, t + CONF_HALO, V7X_LANES), F32),
            pltpu.VMEM((n_slabs, t, V7X_LANES), F32),
        ],
        compiler_params=_params(2),
        name="conformer_layer",
    )(x, g.reshape(1, d), w_in.astype(BF16), cw, cb,
      ln_g.reshape(1, -1), ln_b.reshape(1, -1), w_out.astype(BF16))


def _short_conv_kernel(x_ref, g_ref, w_in_ref, cw_ref, w_out_ref, o_ref, h_ref, u_ref):
    si = pl.program_id(1)
    t = SEQ_TILE
    n_chunks = D_INNER // CHAN_CHUNK

    @pl.when(si == 0)
    def _():
        u_ref[0:SHORT_HALO, :] = jnp.zeros((SHORT_HALO, D_INNER), F32)

    x = x_ref[0]
    h_ref[...] = _rms_norm_rows(x, g_ref[...]).astype(BF16)

    acc = x
    for c in range(n_chunks):
        lo = c * CHAN_CHUNK
        cols = lambda part: slice(part * D_INNER + lo, part * D_INNER + lo + CHAN_CHUNK)
        u = _dot(h_ref[...], w_in_ref[:, cols(0)])
        c_gate = _dot(h_ref[...], w_in_ref[:, cols(2)])
        u_ref[SHORT_HALO:SHORT_HALO + t, lo:lo + CHAN_CHUNK] = c_gate * u
        conv = jnp.zeros((t, CHAN_CHUNK), F32)
        for k in range(SHORT_KERNEL):
            first = SHORT_HALO - (SHORT_KERNEL - 1) + k
            conv = conv + cw_ref[k:k + 1, lo:lo + CHAN_CHUNK] * u_ref[first:first + t, lo:lo + CHAN_CHUNK]
        b_gate = _dot(h_ref[...], w_in_ref[:, cols(1)])
        z = _dot(h_ref[...], w_in_ref[:, cols(3)])
        y = (b_gate * conv * _silu(z)).astype(BF16)
        acc = acc + _dot(y, w_out_ref[lo:lo + CHAN_CHUNK, :])
    o_ref[0] = acc

    u_ref[0:SHORT_HALO, :] = u_ref[t:t + SHORT_HALO, :]


def _short_conv_layer(x, g, w_in, conv_w, w_out):
    b, s, d = x.shape
    t = SEQ_TILE
    return pl.pallas_call(
        _short_conv_kernel,
        grid=(b, s // t),
        in_specs=[
            pl.BlockSpec((1, t, d), lambda bi, si: (bi, si, 0)),
            _resident((1, d)),
            _resident((d, 4 * D_INNER)),
            _resident((SHORT_KERNEL, D_INNER)),
            _resident((D_INNER, d)),
        ],
        out_specs=pl.BlockSpec((1, t, d), lambda bi, si: (bi, si, 0)),
        out_shape=jax.ShapeDtypeStruct(x.shape, F32),
        scratch_shapes=[
            pltpu.VMEM((t, d), BF16),
            pltpu.VMEM((t + SHORT_HALO, D_INNER), F32),
        ],
        compiler_params=_params(2),
        name="short_conv_layer",
    )(x, g.reshape(1, d), w_in.astype(BF16), conv_w, w_out.astype(BF16))


def _fox_proj_kernel(x_ref, g_ref, wqk_ref, wvt_ref, wz_ref, wf_hi_ref, wf_lo_ref, fb_ref,
                     qg_ref, kg_ref, qx_ref, kx_ref, vt_ref, gate_ref, h_ref, carry_ref):
    si = pl.program_id(1)
    t = SEQ_TILE
    dh = FOX_HEAD_DIM

    @pl.when(si == 0)
    def _():
        carry_ref[...] = jnp.zeros_like(carry_ref)

    x = x_ref[0]
    hf = _rms_norm_rows(x, g_ref[...])
    h_ref[...] = hf.astype(BF16)

    h_hi = h_ref[...]
    h_lo = (hf - h_hi.astype(F32)).astype(BF16)
    f_logit = (_dot(h_hi, wf_hi_ref[...]) + _dot(h_hi, wf_lo_ref[...])
               + _dot(h_lo, wf_hi_ref[...])) + fb_ref[...]
    log_f = jnp.minimum(f_logit, 0.0) - jnp.log1p(jnp.exp(-jnp.abs(f_logit)))
    rows = lax.broadcasted_iota(jnp.int32, (t, t), 0)
    cols = lax.broadcasted_iota(jnp.int32, (t, t), 1)
    tri = jnp.where(cols <= rows, 1.0, 0.0).astype(BF16)
    f_hi, f_mid, f_lo = _split3(log_f)
    c = (_dot(tri, f_hi) + _dot(tri, f_mid) + _dot(tri, f_lo)) + carry_ref[...]
    carry_ref[...] = c[t - 1:t, :]
    c_hi, c_mid, c_lo = _split3(c * LOG2_E)
    c_hi, c_mid, c_lo = c_hi.astype(F32), c_mid.astype(F32), c_lo.astype(F32)

    lane = lax.broadcasted_iota(jnp.int32, (t, dh), 1)
    scale = dh ** -0.5 * LOG2_E
    heads_per_chunk = CHAN_CHUNK // dh
    for c4 in range(D_INNER // CHAN_CHUNK):
        lo = c4 * CHAN_CHUNK
        q4 = _dot(h_ref[...], wqk_ref[:, lo:lo + CHAN_CHUNK])
        k4 = _dot(h_ref[...], wqk_ref[:, D_INNER + lo:D_INNER + lo + CHAN_CHUNK])
        for j in range(heads_per_chunk):
            hd = c4 * heads_per_chunk + j
            qh = _rms_norm_rows(q4[:, j * dh:(j + 1) * dh], qg_ref[...]) * scale
            kh = _rms_norm_rows(k4[:, j * dh:(j + 1) * dh], kg_ref[...])
            hi, mid, lw = c_hi[:, hd:hd + 1], c_mid[:, hd:hd + 1], c_lo[:, hd:hd + 1]
            q_ext = jnp.where(lane < 3, 1.0,
                              jnp.where(lane == 3, hi,
                                        jnp.where(lane == 4, mid,
                                                  jnp.where(lane == 5, lw, 0.0))))
            k_ext = jnp.where(lane == 0, -hi,
                              jnp.where(lane == 1, -mid,
                                        jnp.where(lane == 2, -lw,
                                                  jnp.where(lane < 6, 1.0, 0.0))))
            qx_ref[0, hd, :, 0:dh] = qh.astype(BF16)
            qx_ref[0, hd, :, dh:2 * dh] = q_ext.astype(BF16)
            kx_ref[0, hd, :, 0:dh] = kh.astype(BF16)
            kx_ref[0, hd, :, dh:2 * dh] = k_ext.astype(BF16)
        z = _dot(h_ref[...], wz_ref[:, lo:lo + CHAN_CHUNK])
        gate_ref[0, :, lo:lo + CHAN_CHUNK] = _silu(z).astype(BF16)
        vt = _dot_nt(wvt_ref[lo:lo + CHAN_CHUNK, :], h_ref[...])
        for j in range(heads_per_chunk):
            hd = c4 * heads_per_chunk + j
            vt_ref[0, hd, 0] = vt[j * dh:(j + 1) * dh, :].astype(BF16)


def _fox_proj(x, g, w_in, f_bias, q_norm, k_norm):
    b, s, d = x.shape
    t = SEQ_TILE
    h, dh = FOX_HEADS, FOX_HEAD_DIM
    e = D_INNER
    wqk = w_in[:, :2 * e].astype(BF16)
    wvt = w_in[:, 2 * e:3 * e].T.astype(BF16)
    wz = w_in[:, 3 * e:4 * e].astype(BF16)
    wf = w_in[:, 4 * e:]
    wf_hi = wf.astype(BF16)
    wf_lo = (wf - wf_hi.astype(F32)).astype(BF16)
    return pl.pallas_call(
        _fox_proj_kernel,
        grid=(b, s // t),
        in_specs=[
            pl.BlockSpec((1, t, d), lambda bi, si: (bi, si, 0)),
            _resident((1, d)),
            _resident((d, 2 * e)),
            _resident((e, d)),
            _resident((d, e)),
            _resident((d, h)), _resident((d, h)), _resident((1, h)),
            _resident((1, dh)), _resident((1, dh)),
        ],
        out_specs=[
            pl.BlockSpec((1, h, t, QK_WIDTH), lambda bi, si: (bi, 0, si, 0)),
            pl.BlockSpec((1, h, t, QK_WIDTH), lambda bi, si: (bi, 0, si, 0)),
            pl.BlockSpec((1, h, 1, dh, t), lambda bi, si: (bi, 0, si, 0, 0)),
            pl.BlockSpec((1, t, e), lambda bi, si: (bi, si, 0)),
        ],
        out_shape=[
            jax.ShapeDtypeStruct((b, h, s, QK_WIDTH), BF16),
            jax.ShapeDtypeStruct((b, h, s, QK_WIDTH), BF16),
            jax.ShapeDtypeStruct((b, h, s // t, dh, t), BF16),
            jax.ShapeDtypeStruct((b, s, e), BF16),
        ],
        scratch_shapes=[
            pltpu.VMEM((t, d), BF16),
            pltpu.VMEM((1, h), F32),
        ],
        compiler_params=_params(2),
        name="fox_proj",
    )(x, g.reshape(1, d), wqk, wvt, wz, wf_hi, wf_lo, f_bias.reshape(1, h),
      q_norm.reshape(1, dh), k_norm.reshape(1, dh))


def _fox_attn_kernel(qx_ref, kx_ref, vt_ref, gate_ref, y_ref,
                     s0_ref, s1_ref, p0_ref, p1_ref, a0_ref, a1_ref, m_ref, l_ref, acc_ref):
    qi = pl.program_id(2)
    kt = KV_TILE

    def scores(blk, s_ref, c0, c1):
        k0 = pl.multiple_of(blk * kt, kt)
        s_ref[:, c0:c1] = _dot_nt(kx_ref[0, 0, pl.ds(k0, kt), :], qx_ref[0, 0, c0:c1, :])

    def softmax(s_ref, p_ref, a_ref, c0, c1, masked):
        s = s_ref[:, c0:c1]
        if masked:
            kv_pos = lax.broadcasted_iota(jnp.int32, s.shape, 0)
            q_pos = lax.broadcasted_iota(jnp.int32, s.shape, 1)
            s = jnp.where(kv_pos <= q_pos, s, -jnp.inf)
        m_prev = m_ref[:, c0:c1]
        m_new = jnp.maximum(m_prev, jnp.max(s, axis=0, keepdims=True))
        p = jnp.exp2(s - m_new)
        alpha = jnp.exp2(m_prev - m_new)
        l_ref[:, c0:c1] = alpha * l_ref[:, c0:c1] + jnp.sum(p, axis=0, keepdims=True)
        m_ref[:, c0:c1] = m_new
        a_ref[:, c0:c1] = alpha
        p_ref[:, c0:c1] = p.astype(BF16)

    def accumulate(blk, p_ref, a_ref, c0, c1):
        acc_ref[:, c0:c1] = (a_ref[:, c0:c1] * acc_ref[:, c0:c1]
                             + _dot(vt_ref[0, 0, blk], p_ref[:, c0:c1]))

    m_ref[...] = jnp.full(m_ref.shape, -1e30, F32)
    l_ref[...] = jnp.zeros(l_ref.shape, F32)
    acc_ref[...] = jnp.zeros(acc_ref.shape, F32)
    p1_ref[...] = jnp.zeros(p1_ref.shape, BF16)
    a1_ref[...] = jnp.ones(a1_ref.shape, F32)

    full = (0, Q_TILE)
    scores(0, s0_ref, *full)

    def pair(t, carry):
        even = 2 * t
        scores(even + 1, s1_ref, *full)
        softmax(s0_ref, p0_ref, a0_ref, *full, masked=False)
        accumulate(jnp.maximum(even - 1, 0), p1_ref, a1_ref, *full)
        scores(even + 2, s0_ref, *full)
        softmax(s1_ref, p1_ref, a1_ref, *full, masked=False)
        accumulate(even, p0_ref, a0_ref, *full)
        return carry

    lax.fori_loop(0, qi, pair, 0)

    d0 = 2 * qi
    left, right = (0, kt), (kt, Q_TILE)
    scores(d0 + 1, s1_ref, *right)
    softmax(s0_ref, p0_ref, a0_ref, *left, masked=True)
    softmax(s0_ref, p0_ref, a0_ref, *right, masked=False)
    accumulate(jnp.maximum(d0 - 1, 0), p1_ref, a1_ref, *full)
    softmax(s1_ref, p1_ref, a1_ref, *right, masked=True)
    accumulate(d0, p0_ref, a0_ref, *full)
    accumulate(d0 + 1, p1_ref, a1_ref, *right)

    o_t = acc_ref[...] * (1.0 / l_ref[...])
    y_ref[0] = (o_t.T * gate_ref[0].astype(F32)).astype(BF16)


def _fox_attn_bounded_kernel(qx_ref, kx_ref, vt_ref, gate_ref, y_ref,
                             p0_ref, p1_ref, l_ref, acc_ref):
    qi = pl.program_id(2)
    kt = KV_TILE

    def probs(blk, p_ref, c0, c1, masked):
        k0 = pl.multiple_of(blk * kt, kt)
        s = _dot_nt(kx_ref[0, 0, pl.ds(k0, kt), :], qx_ref[0, 0, c0:c1, :])
        if masked:
            kv_pos = lax.broadcasted_iota(jnp.int32, s.shape, 0)
            q_pos = lax.broadcasted_iota(jnp.int32, s.shape, 1)
            s = jnp.where(kv_pos <= q_pos, s, -jnp.inf)
        p = jnp.exp2(s)
        l_ref[:, c0:c1] = l_ref[:, c0:c1] + jnp.sum(p, axis=0, keepdims=True)
        p_ref[:, c0:c1] = p.astype(BF16)

    def accumulate(blk, p_ref, c0, c1):
        acc_ref[:, c0:c1] = acc_ref[:, c0:c1] + _dot(vt_ref[0, 0, blk], p_ref[:, c0:c1])

    l_ref[...] = jnp.zeros(l_ref.shape, F32)
    acc_ref[...] = jnp.zeros(acc_ref.shape, F32)
    p1_ref[...] = jnp.zeros(p1_ref.shape, BF16)

    full = (0, Q_TILE)

    def pair(t, carry):
        even = 2 * t
        probs(even, p0_ref, *full, masked=False)
        accumulate(jnp.maximum(even - 1, 0), p1_ref, *full)
        probs(even + 1, p1_ref, *full, masked=False)
        accumulate(even, p0_ref, *full)
        return carry

    lax.fori_loop(0, qi, pair, 0)

    d0 = 2 * qi
    left, right = (0, kt), (kt, Q_TILE)
    probs(d0, p0_ref, *left, masked=True)
    probs(d0, p0_ref, *right, masked=False)
    accumulate(jnp.maximum(d0 - 1, 0), p1_ref, *full)
    probs(d0 + 1, p1_ref, *right, masked=True)
    accumulate(d0, p0_ref, *full)
    accumulate(d0 + 1, p1_ref, *right)

    o_t = acc_ref[...] * (1.0 / l_ref[...])
    y_ref[0] = (o_t.T * gate_ref[0].astype(F32)).astype(BF16)


def _fox_attention(qx, kx, vt, gate, *, bounded):
    b, h, s, _ = qx.shape
    dh = FOX_HEAD_DIM
    assert KV_TILE == SEQ_TILE and Q_TILE == 2 * KV_TILE
    probs = pltpu.VMEM((KV_TILE, Q_TILE), BF16)
    row = pltpu.VMEM((1, Q_TILE), F32)
    acc = pltpu.VMEM((dh, Q_TILE), F32)
    if bounded:
        body, scratch = _fox_attn_bounded_kernel, [probs, probs, row, acc]
    else:
        scores = pltpu.VMEM((KV_TILE, Q_TILE), F32)
        body, scratch = _fox_attn_kernel, [scores, scores, probs, probs, row, row, row, row, acc]
    return pl.pallas_call(
        body,
        grid=(b, h, s // Q_TILE),
        in_specs=[
            pl.BlockSpec((1, 1, Q_TILE, QK_WIDTH), lambda bi, hi, qi: (bi, hi, qi, 0)),
            pl.BlockSpec((1, 1, s, QK_WIDTH), lambda bi, hi, qi: (bi, hi, 0, 0)),
            pl.BlockSpec((1, 1, s // KV_TILE, dh, KV_TILE), lambda bi, hi, qi: (bi, hi, 0, 0, 0)),
            pl.BlockSpec((1, Q_TILE, dh), lambda bi, hi, qi: (bi, qi, hi)),
        ],
        out_specs=pl.BlockSpec((1, Q_TILE, dh), lambda bi, hi, qi: (bi, qi, hi)),
        out_shape=jax.ShapeDtypeStruct((b, s, h * dh), BF16),
        scratch_shapes=scratch,
        compiler_params=_params(3),
        name="fox_attention_bounded" if bounded else "fox_attention",
    )(qx, kx, vt, gate)


def _out_proj_kernel(x_ref, y_ref, w_ref, o_ref):
    o_ref[0] = x_ref[0] + _dot(y_ref[0], w_ref[...])


def _out_proj(x, y, w_out):
    b, s, d = x.shape
    t = SEQ_TILE
    e = y.shape[-1]
    return pl.pallas_call(
        _out_proj_kernel,
        grid=(b, s // t),
        in_specs=[
            pl.BlockSpec((1, t, d), lambda bi, si: (bi, si, 0)),
            pl.BlockSpec((1, t, e), lambda bi, si: (bi, si, 0)),
            _resident((e, d)),
        ],
        out_specs=pl.BlockSpec((1, t, d), lambda bi, si: (bi, si, 0)),
        out_shape=jax.ShapeDtypeStruct(x.shape, F32),
        compiler_params=_params(2),
        name="fox_out_proj",
    )(x, y, w_out.astype(BF16))


def _fox_layer(x, g, w_in, f_bias, q_norm, k_norm, w_out):
    qx, kx, vt, gate = _fox_proj(x, g, w_in, f_bias, q_norm, k_norm)
    qk_bound = (1.01 * FOX_HEAD_DIM ** 0.5 * LOG2_E) * jnp.max(jnp.abs(q_norm)) * jnp.max(jnp.abs(k_norm))
    y = lax.cond(qk_bound <= BOUNDED_LOGIT_LOG2,
                 functools.partial(_fox_attention, bounded=True),
                 functools.partial(_fox_attention, bounded=False),
                 qx, kx, vt, gate)
    return _out_proj(x, y, w_out)


def kernel(x, a_norm, a_w_in, a_conv_w, a_conv_b, a_ln_g, a_ln_b, a_w_out, b_norm, b_w_in, b_f_bias, b_q_norm, b_k_norm, b_w_out, c_norm, c_w_in, c_conv_w, c_w_out):
    depth = a_norm.shape[0] + b_norm.shape[0] + c_norm.shape[0]
    for i in range(depth):
        kind, j = i % 3, i // 3
        if kind == 0:
            x = _conformer_layer(x, a_norm[j], a_w_in[j], a_conv_w[j], a_conv_b[j],
                                 a_ln_g[j], a_ln_b[j], a_w_out[j])
        elif kind == 1:
            x = _fox_layer(x, b_norm[j], b_w_in[j], b_f_bias[j], b_q_norm[j], b_k_norm[j],
                           b_w_out[j])
        else:
            x = _short_conv_layer(x, c_norm[j], c_w_in[j], c_conv_w[j], c_w_out[j])
    return x
```

```python
import functools

import jax
import jax.numpy as jnp
from jax import lax
from jax.experimental import pallas as pl
from jax.experimental.pallas import tpu as pltpu

D_MODEL = 1024
D_INNER = 2048
CONF_KERNEL = 31
SHORT_KERNEL = 3
FOX_HEADS = 16
FOX_HEAD_DIM = 128
NORM_EPS = 1e-6

V7X_LANES = 128
V7X_SUBLANES = 8
V7X_VMEM_LIMIT_BYTES = 58 * 1024 * 1024

SEQ_TILE = 512
CHAN_CHUNK = 512
CONF_HALO = 32
SHORT_HALO = 8
CONV_ROWS = 32
CONF_HALF = 256
PROJ_ROWS = 128
KV_TILE = 512
Q_TILE = 2 * KV_TILE
Q_TILE_BOUNDED = 4 * KV_TILE
QK_WIDTH = 2 * FOX_HEAD_DIM
LOG2_E = 1.4426950408889634
BOUNDED_LOGIT_LOG2 = 50.0

BF16 = jnp.bfloat16
F32 = jnp.float32


def _sigmoid(x):
    return 1.0 / (1.0 + jnp.exp(-x))


def _silu(x):
    return x * _sigmoid(x)


def _rms_norm_rows(x, g):
    ms = jnp.mean(x * x, axis=-1, keepdims=True)
    return x * lax.rsqrt(ms + NORM_EPS) * g


def _dot(a, b):
    return jnp.dot(a, b, preferred_element_type=F32)


def _dot_nt(a, b):
    return lax.dot_general(a, b, (((1,), (1,)), ((), ())), preferred_element_type=F32)


def _split3(x):
    hi = x.astype(BF16)
    r = x - hi.astype(F32)
    mid = r.astype(BF16)
    lo = (r - mid.astype(F32)).astype(BF16)
    return hi, mid, lo


def _resident(shape):
    zeros = (0,) * len(shape)
    return pl.BlockSpec(shape, lambda *_: zeros, pipeline_mode=pl.Buffered(1))


def _params(n_axes):
    return pltpu.CompilerParams(
        dimension_semantics=("arbitrary",) * n_axes,
        vmem_limit_bytes=V7X_VMEM_LIMIT_BYTES,
    )


def _conformer_kernel(x_ref, g_ref, w_in_ref, cw_ref, cb_ref, lg_ref, lb_ref, w_out_ref,
                      o_ref, h_ref, ua_ref, ub_ref, halo_ref, c_ref, zg_ref):
    si = pl.program_id(1)
    t = SEQ_TILE
    n_slabs = D_INNER // V7X_LANES
    n_half = D_INNER // CONF_HALF
    slabs_per_half = CONF_HALF // V7X_LANES
    first = CONF_HALO - (CONF_KERNEL - 1)

    @pl.when(si == 0)
    def _():
        halo_ref[...] = jnp.zeros(halo_ref.shape, F32)

    x = x_ref[0]
    h_ref[...] = _rms_norm_rows(x, g_ref[...]).astype(BF16)

    def project_steps(hc, u_ref):
        def rows(q):
            r = slice(q * PROJ_ROWS, (q + 1) * PROJ_ROWS)
            val = _dot(h_ref[r, :], w_in_ref[hc])
            gate = _dot(h_ref[r, :], w_in_ref[n_half + hc])
            u = val * _sigmoid(gate)
            for j in range(slabs_per_half):
                u_ref[j, CONF_HALO + q * PROJ_ROWS:CONF_HALO + (q + 1) * PROJ_ROWS, :] = (
                    u[:, j * V7X_LANES:(j + 1) * V7X_LANES])
            z = _dot(h_ref[r, :], w_in_ref[2 * n_half + hc])
            zg_ref[hc, r, :] = _silu(z)
        return [functools.partial(rows, q) for q in range(t // PROJ_ROWS)]

    def conv_steps(hc, u_ref):
        def block(j, rb):
            sl = hc * slabs_per_half + j
            if rb == 0:
                u_ref[j, 0:CONF_HALO, :] = halo_ref[sl]
            r0 = rb * CONV_ROWS
            acc = jnp.zeros((CONV_ROWS, V7X_LANES), F32)
            for k in range(CONF_KERNEL):
                acc = acc + (cw_ref[sl, k:k + 1, :]
                             * u_ref[j, r0 + first + k:r0 + first + k + CONV_ROWS, :])
            c_ref[sl, r0:r0 + CONV_ROWS, :] = acc + cb_ref[sl]
            if rb == t // CONV_ROWS - 1:
                halo_ref[sl] = u_ref[j, t:t + CONF_HALO, :]
        return [functools.partial(block, j, rb)
                for j in range(slabs_per_half) for rb in range(t // CONV_ROWS)]

    def interleave(a_steps, b_steps):
        per_a = -(-len(b_steps) // max(len(a_steps), 1))
        for i, a in enumerate(a_steps):
            a()
            for b in b_steps[i * per_a:(i + 1) * per_a]:
                b()

    interleave(project_steps(0, ua_ref), [])

    def pair(i, carry):
        even = 2 * i
        interleave(project_steps(even + 1, ub_ref), conv_steps(even, ua_ref))
        interleave(project_steps(even + 2, ua_ref), conv_steps(even + 1, ub_ref))
        return carry

    lax.fori_loop(0, n_half // 2 - 1, pair, 0)
    interleave(project_steps(n_half - 1, ub_ref), conv_steps(n_half - 2, ua_ref))
    for step in conv_steps(n_half - 1, ub_ref):
        step()

    row_sum = c_ref[0]
    for sl in range(1, n_slabs):
        row_sum = row_sum + c_ref[sl]
    mu = jnp.sum(row_sum, axis=-1, keepdims=True) * (1.0 / D_INNER)
    sq_sum = jnp.square(c_ref[0] - mu)
    for sl in range(1, n_slabs):
        sq_sum = sq_sum + jnp.square(c_ref[sl] - mu)
    var = jnp.sum(sq_sum, axis=-1, keepdims=True) * (1.0 / D_INNER)
    rstd = lax.rsqrt(var + NORM_EPS)

    slabs_per_chunk = CHAN_CHUNK // V7X_LANES
    halves_per_chunk = CHAN_CHUNK // CONF_HALF
    acc = x
    for c in range(D_INNER // CHAN_CHUNK):
        lo = c * CHAN_CHUNK
        conv_out = jnp.concatenate(
            [c_ref[c * slabs_per_chunk + j] for j in range(slabs_per_chunk)], axis=-1)
        z_gate = jnp.concatenate(
            [zg_ref[c * halves_per_chunk + j] for j in range(halves_per_chunk)], axis=-1)
        ln = (conv_out - mu) * rstd * lg_ref[:, lo:lo + CHAN_CHUNK] + lb_ref[:, lo:lo + CHAN_CHUNK]
        y = (_silu(ln) * z_gate).astype(BF16)
        acc = acc + _dot(y, w_out_ref[lo:lo + CHAN_CHUNK, :])
    o_ref[0] = acc


def _conformer_layer(x, g, w_in, conv_w, conv_b, ln_g, ln_b, w_out):
    b, s, d = x.shape
    t = SEQ_TILE
    row = lambda n: _resident((1, n))
    n_slabs = D_INNER // V7X_LANES
    n_half = D_INNER // CONF_HALF
    slabs_per_half = CONF_HALF // V7X_LANES
    w3 = (w_in.reshape(d, 3, n_half, CONF_HALF).transpose(1, 2, 0, 3)
          .reshape(3 * n_half, d, CONF_HALF).astype(BF16))
    cw = conv_w.reshape(CONF_KERNEL, n_slabs, V7X_LANES).transpose(1, 0, 2)
    cb = conv_b.reshape(n_slabs, 1, V7X_LANES)
    u_buf = pltpu.VMEM((slabs_per_half, t + CONF_HALO, V7X_LANES), F32)
    return pl.pallas_call(
        _conformer_kernel,
        grid=(b, s // t),
        in_specs=[
            pl.BlockSpec((1, t, d), lambda bi, si: (bi, si, 0)),
            row(d),
            _resident((3 * n_half, d, CONF_HALF)),
            _resident((n_slabs, CONF_KERNEL, V7X_LANES)),
            _resident((n_slabs, 1, V7X_LANES)),
            row(D_INNER), row(D_INNER),
            _resident((D_INNER, d)),
        ],
        out_specs=pl.BlockSpec((1, t, d), lambda bi, si: (bi, si, 0)),
        out_shape=jax.ShapeDtypeStruct(x.shape, F32),
        scratch_shapes=[
            pltpu.VMEM((t, d), BF16),
            u_buf, u_buf,
            pltpu.VMEM((n_slabs, CONF_HALO, V7X_LANES), F32),
            pltpu.VMEM((n_slabs, t, V7X_LANES), F32),
            pltpu.VMEM((n_half, t, CONF_HALF), F32),
        ],
        compiler_params=_params(2),
        name="conformer_layer",
    )(x, g.reshape(1, d), w3, cw, cb,
      ln_g.reshape(1, -1), ln_b.reshape(1, -1), w_out.astype(BF16))


def _short_conv_kernel(x_ref, g_ref, w_in_ref, cw_ref, w_out_ref, o_ref, h_ref, u_ref):
    si = pl.program_id(1)
    t = SEQ_TILE
    n_chunks = D_INNER // CHAN_CHUNK

    @pl.when(si == 0)
    def _():
        u_ref[0:SHORT_HALO, :] = jnp.zeros((SHORT_HALO, D_INNER), F32)

    x = x_ref[0]
    h_ref[...] = _rms_norm_rows(x, g_ref[...]).astype(BF16)

    acc = x
    for c in range(n_chunks):
        lo = c * CHAN_CHUNK
        cols = lambda part: slice(part * D_INNER + lo, part * D_INNER + lo + CHAN_CHUNK)
        u = _dot(h_ref[...], w_in_ref[:, cols(0)])
        c_gate = _dot(h_ref[...], w_in_ref[:, cols(2)])
        u_ref[SHORT_HALO:SHORT_HALO + t, lo:lo + CHAN_CHUNK] = c_gate * u
        conv = jnp.zeros((t, CHAN_CHUNK), F32)
        for k in range(SHORT_KERNEL):
            first = SHORT_HALO - (SHORT_KERNEL - 1) + k
            conv = conv + cw_ref[k:k + 1, lo:lo + CHAN_CHUNK] * u_ref[first:first + t, lo:lo + CHAN_CHUNK]
        b_gate = _dot(h_ref[...], w_in_ref[:, cols(1)])
        z = _dot(h_ref[...], w_in_ref[:, cols(3)])
        y = (b_gate * conv * _silu(z)).astype(BF16)
        acc = acc + _dot(y, w_out_ref[lo:lo + CHAN_CHUNK, :])
    o_ref[0] = acc

    u_ref[0:SHORT_HALO, :] = u_ref[t:t + SHORT_HALO, :]


def _short_conv_layer(x, g, w_in, conv_w, w_out):
    b, s, d = x.shape
    t = SEQ_TILE
    return pl.pallas_call(
        _short_conv_kernel,
        grid=(b, s // t),
        in_specs=[
            pl.BlockSpec((1, t, d), lambda bi, si: (bi, si, 0)),
            _resident((1, d)),
            _resident((d, 4 * D_INNER)),
            _resident((SHORT_KERNEL, D_INNER)),
            _resident((D_INNER, d)),
        ],
        out_specs=pl.BlockSpec((1, t, d), lambda bi, si: (bi, si, 0)),
        out_shape=jax.ShapeDtypeStruct(x.shape, F32),
        scratch_shapes=[
            pltpu.VMEM((t, d), BF16),
            pltpu.VMEM((t + SHORT_HALO, D_INNER), F32),
        ],
        compiler_params=_params(2),
        name="short_conv_layer",
    )(x, g.reshape(1, d), w_in.astype(BF16), conv_w, w_out.astype(BF16))


def _fox_proj_kernel(x_ref, g_ref, wqk_ref, wvt_ref, wz_ref, wf_hi_ref, wf_lo_ref, fb_ref,
                     qg_ref, kg_ref, qx_ref, kx_ref, vt_ref, gate_ref, h_ref, carry_ref):
    si = pl.program_id(1)
    t = SEQ_TILE
    dh = FOX_HEAD_DIM

    @pl.when(si == 0)
    def _():
        carry_ref[...] = jnp.zeros_like(carry_ref)

    x = x_ref[0]
    hf = _rms_norm_rows(x, g_ref[...])
    h_ref[...] = hf.astype(BF16)

    h_hi = h_ref[...]
    h_lo = (hf - h_hi.astype(F32)).astype(BF16)
    f_logit = (_dot(h_hi, wf_hi_ref[...]) + _dot(h_hi, wf_lo_ref[...])
               + _dot(h_lo, wf_hi_ref[...])) + fb_ref[...]
    log_f = jnp.minimum(f_logit, 0.0) - jnp.log1p(jnp.exp(-jnp.abs(f_logit)))
    rows = lax.broadcasted_iota(jnp.int32, (t, t), 0)
    cols = lax.broadcasted_iota(jnp.int32, (t, t), 1)
    tri = jnp.where(cols <= rows, 1.0, 0.0).astype(BF16)
    f_hi, f_mid, f_lo = _split3(log_f)
    c = (_dot(tri, f_hi) + _dot(tri, f_mid) + _dot(tri, f_lo)) + carry_ref[...]
    carry_ref[...] = c[t - 1:t, :]
    c_hi, c_mid, c_lo = _split3(c * LOG2_E)
    c_hi, c_mid, c_lo = c_hi.astype(F32), c_mid.astype(F32), c_lo.astype(F32)

    lane = lax.broadcasted_iota(jnp.int32, (t, dh), 1)
    scale = dh ** -0.5 * LOG2_E
    heads_per_chunk = CHAN_CHUNK // dh
    for c4 in range(D_INNER // CHAN_CHUNK):
        lo = c4 * CHAN_CHUNK
        q4 = _dot(h_ref[...], wqk_ref[:, lo:lo + CHAN_CHUNK])
        k4 = _dot(h_ref[...], wqk_ref[:, D_INNER + lo:D_INNER + lo + CHAN_CHUNK])
        for j in range(heads_per_chunk):
            hd = c4 * heads_per_chunk + j
            qh = _rms_norm_rows(q4[:, j * dh:(j + 1) * dh], qg_ref[...]) * scale
            kh = _rms_norm_rows(k4[:, j * dh:(j + 1) * dh], kg_ref[...])
            hi, mid, lw = c_hi[:, hd:hd + 1], c_mid[:, hd:hd + 1], c_lo[:, hd:hd + 1]
            q_ext = jnp.where(lane < 3, 1.0,
                              jnp.where(lane == 3, hi,
                                        jnp.where(lane == 4, mid,
                                                  jnp.where(lane == 5, lw, 0.0))))
            k_ext = jnp.where(lane == 0, -hi,
                              jnp.where(lane == 1, -mid,
                                        jnp.where(lane == 2, -lw,
                                                  jnp.where(lane < 6, 1.0, 0.0))))
            qx_ref[0, hd, :, 0:dh] = qh.astype(BF16)
            qx_ref[0, hd, :, dh:2 * dh] = q_ext.astype(BF16)
            kx_ref[0, hd, :, 0:dh] = kh.astype(BF16)
            kx_ref[0, hd, :, dh:2 * dh] = k_ext.astype(BF16)
        z = _dot(h_ref[...], wz_ref[:, lo:lo + CHAN_CHUNK])
        gate_ref[0, :, lo:lo + CHAN_CHUNK] = _silu(z).astype(BF16)
        vt = _dot_nt(wvt_ref[lo:lo + CHAN_CHUNK, :], h_ref[...])
        for j in range(heads_per_chunk):
            hd = c4 * heads_per_chunk + j
            vt_ref[0, hd, 0] = vt[j * dh:(j + 1) * dh, :].astype(BF16)


def _fox_proj(x, g, w_in, f_bias, q_norm, k_norm):
    b, s, d = x.shape
    t = SEQ_TILE
    h, dh = FOX_HEADS, FOX_HEAD_DIM
    e = D_INNER
    wqk = w_in[:, :2 * e].astype(BF16)
    wvt = w_in[:, 2 * e:3 * e].T.astype(BF16)
    wz = w_in[:, 3 * e:4 * e].astype(BF16)
    wf = w_in[:, 4 * e:]
    wf_hi = wf.astype(BF16)
    wf_lo = (wf - wf_hi.astype(F32)).astype(BF16)
    return pl.pallas_call(
        _fox_proj_kernel,
        grid=(b, s // t),
        in_specs=[
            pl.BlockSpec((1, t, d), lambda bi, si: (bi, si, 0)),
            _resident((1, d)),
            _resident((d, 2 * e)),
            _resident((e, d)),
            _resident((d, e)),
            _resident((d, h)), _resident((d, h)), _resident((1, h)),
            _resident((1, dh)), _resident((1, dh)),
        ],
        out_specs=[
            pl.BlockSpec((1, h, t, QK_WIDTH), lambda bi, si: (bi, 0, si, 0)),
            pl.BlockSpec((1, h, t, QK_WIDTH), lambda bi, si: (bi, 0, si, 0)),
            pl.BlockSpec((1, h, 1, dh, t), lambda bi, si: (bi, 0, si, 0, 0)),
            pl.BlockSpec((1, t, e), lambda bi, si: (bi, si, 0)),
        ],
        out_shape=[
            jax.ShapeDtypeStruct((b, h, s, QK_WIDTH), BF16),
            jax.ShapeDtypeStruct((b, h, s, QK_WIDTH), BF16),
            jax.ShapeDtypeStruct((b, h, s // t, dh, t), BF16),
            jax.ShapeDtypeStruct((b, s, e), BF16),
        ],
        scratch_shapes=[
            pltpu.VMEM((t, d), BF16),
            pltpu.VMEM((1, h), F32),
        ],
        compiler_params=_params(2),
        name="fox_proj",
    )(x, g.reshape(1, d), wqk, wvt, wz, wf_hi, wf_lo, f_bias.reshape(1, h),
      q_norm.reshape(1, dh), k_norm.reshape(1, dh))


def _fox_attn_kernel(qx_ref, kx_ref, vt_ref, gate_ref, y_ref,
                     s0_ref, s1_ref, p0_ref, p1_ref, a0_ref, a1_ref, m_ref, l_ref, acc_ref):
    qi = pl.program_id(2)
    kt = KV_TILE

    def scores(blk, s_ref, c0, c1):
        k0 = pl.multiple_of(blk * kt, kt)
        s_ref[:, c0:c1] = _dot_nt(kx_ref[0, 0, pl.ds(k0, kt), :], qx_ref[0, 0, c0:c1, :])

    def softmax(s_ref, p_ref, a_ref, c0, c1, masked):
        s = s_ref[:, c0:c1]
        if masked:
            kv_pos = lax.broadcasted_iota(jnp.int32, s.shape, 0)
            q_pos = lax.broadcasted_iota(jnp.int32, s.shape, 1)
            s = jnp.where(kv_pos <= q_pos, s, -jnp.inf)
        m_prev = m_ref[:, c0:c1]
        m_new = jnp.maximum(m_prev, jnp.max(s, axis=0, keepdims=True))
        p = jnp.exp2(s - m_new)
        alpha = jnp.exp2(m_prev - m_new)
        l_ref[:, c0:c1] = alpha * l_ref[:, c0:c1] + jnp.sum(p, axis=0, keepdims=True)
        m_ref[:, c0:c1] = m_new
        a_ref[:, c0:c1] = alpha
        p_ref[:, c0:c1] = p.astype(BF16)

    def accumulate(blk, p_ref, a_ref, c0, c1):
        acc_ref[:, c0:c1] = (a_ref[:, c0:c1] * acc_ref[:, c0:c1]
                             + _dot(vt_ref[0, 0, blk], p_ref[:, c0:c1]))

    m_ref[...] = jnp.full(m_ref.shape, -1e30, F32)
    l_ref[...] = jnp.zeros(l_ref.shape, F32)
    acc_ref[...] = jnp.zeros(acc_ref.shape, F32)
    p1_ref[...] = jnp.zeros(p1_ref.shape, BF16)
    a1_ref[...] = jnp.ones(a1_ref.shape, F32)

    full = (0, Q_TILE)
    scores(0, s0_ref, *full)

    def pair(t, carry):
        even = 2 * t
        scores(even + 1, s1_ref, *full)
        softmax(s0_ref, p0_ref, a0_ref, *full, masked=False)
        accumulate(jnp.maximum(even - 1, 0), p1_ref, a1_ref, *full)
        scores(even + 2, s0_ref, *full)
        softmax(s1_ref, p1_ref, a1_ref, *full, masked=False)
        accumulate(even, p0_ref, a0_ref, *full)
        return carry

    lax.fori_loop(0, qi, pair, 0)

    d0 = 2 * qi
    left, right = (0, kt), (kt, Q_TILE)
    scores(d0 + 1, s1_ref, *right)
    softmax(s0_ref, p0_ref, a0_ref, *left, masked=True)
    softmax(s0_ref, p0_ref, a0_ref, *right, masked=False)
    accumulate(jnp.maximum(d0 - 1, 0), p1_ref, a1_ref, *full)
    softmax(s1_ref, p1_ref, a1_ref, *right, masked=True)
    accumulate(d0, p0_ref, a0_ref, *full)
    accumulate(d0 + 1, p1_ref, a1_ref, *right)

    o_t = acc_ref[...] * (1.0 / l_ref[...])
    y_ref[0] = (o_t.T * gate_ref[0].astype(F32)).astype(BF16)


def _fox_attn_bounded_kernel(qx_ref, kx_ref, vt_ref, gate_ref, y_ref,
                             p0_ref, p1_ref, l_ref, acc_ref):
    qi = pl.program_id(2)
    kt = KV_TILE

    def probs(blk, p_ref, c0, c1, masked):
        k0 = pl.multiple_of(blk * kt, kt)
        s = _dot_nt(kx_ref[0, 0, pl.ds(k0, kt), :], qx_ref[0, 0, c0:c1, :])
        if masked:
            kv_pos = lax.broadcasted_iota(jnp.int32, s.shape, 0)
            q_pos = lax.broadcasted_iota(jnp.int32, s.shape, 1)
            s = jnp.where(kv_pos <= q_pos, s, -jnp.inf)
        p = jnp.exp2(s)
        l_ref[:, c0:c1] = l_ref[:, c0:c1] + jnp.sum(p, axis=0, keepdims=True)
        p_ref[:, c0:c1] = p.astype(BF16)

    def accumulate(blk, p_ref, c0, c1):
        acc_ref[:, c0:c1] = acc_ref[:, c0:c1] + _dot(vt_ref[0, 0, blk], p_ref[:, c0:c1])

    l_ref[...] = jnp.zeros(l_ref.shape, F32)
    acc_ref[...] = jnp.zeros(acc_ref.shape, F32)
    p1_ref[...] = jnp.zeros(p1_ref.shape, BF16)

    q_tile = p0_ref.shape[1]
    n_diag = q_tile // kt
    p_refs = (p0_ref, p1_ref)
    full = (0, q_tile)

    def pair(t, carry):
        even = 2 * t
        probs(even, p0_ref, *full, masked=False)
        accumulate(jnp.maximum(even - 1, 0), p1_ref, *full)
        probs(even + 1, p1_ref, *full, masked=False)
        accumulate(even, p0_ref, *full)
        return carry

    d0 = n_diag * qi
    lax.fori_loop(0, d0 // 2, pair, 0)

    def probs_diag(i):
        probs(d0 + i, p_refs[i % 2], i * kt, (i + 1) * kt, masked=True)
        if i + 1 < n_diag:
            probs(d0 + i, p_refs[i % 2], (i + 1) * kt, q_tile, masked=False)

    probs_diag(0)
    accumulate(jnp.maximum(d0 - 1, 0), p1_ref, *full)
    for i in range(1, n_diag):
        probs_diag(i)
        accumulate(d0 + i - 1, p_refs[(i - 1) % 2], (i - 1) * kt, q_tile)
    accumulate(d0 + n_diag - 1, p_refs[(n_diag - 1) % 2], (n_diag - 1) * kt, q_tile)

    o_t = acc_ref[...] * (1.0 / l_ref[...])
    y_ref[0] = (o_t.T * gate_ref[0].astype(F32)).astype(BF16)


def _fox_attention(qx, kx, vt, gate, *, bounded):
    b, h, s, _ = qx.shape
    dh = FOX_HEAD_DIM
    assert KV_TILE == SEQ_TILE and Q_TILE == 2 * KV_TILE and Q_TILE_BOUNDED % (2 * KV_TILE) == 0
    q_tile = Q_TILE_BOUNDED if bounded else Q_TILE
    probs = pltpu.VMEM((KV_TILE, q_tile), BF16)
    row = pltpu.VMEM((1, q_tile), F32)
    acc = pltpu.VMEM((dh, q_tile), F32)
    if bounded:
        body, scratch = _fox_attn_bounded_kernel, [probs, probs, row, acc]
    else:
        scores = pltpu.VMEM((KV_TILE, q_tile), F32)
        body, scratch = _fox_attn_kernel, [scores, scores, probs, probs, row, row, row, row, acc]
    return pl.pallas_call(
        body,
        grid=(b, h, s // q_tile),
        in_specs=[
            pl.BlockSpec((1, 1, q_tile, QK_WIDTH), lambda bi, hi, qi: (bi, hi, qi, 0)),
            pl.BlockSpec((1, 1, s, QK_WIDTH), lambda bi, hi, qi: (bi, hi, 0, 0)),
            pl.BlockSpec((1, 1, s // KV_TILE, dh, KV_TILE), lambda bi, hi, qi: (bi, hi, 0, 0, 0)),
            pl.BlockSpec((1, q_tile, dh), lambda bi, hi, qi: (bi, qi, hi)),
        ],
        out_specs=pl.BlockSpec((1, q_tile, dh), lambda bi, hi, qi: (bi, qi, hi)),
        out_shape=jax.ShapeDtypeStruct((b, s, h * dh), BF16),
        scratch_shapes=scratch,
        compiler_params=_params(3),
        name="fox_attention_bounded" if bounded else "fox_attention",
    )(qx, kx, vt, gate)


def _out_proj_kernel(x_ref, y_ref, w_ref, o_ref):
    o_ref[0] = x_ref[0] + _dot(y_ref[0], w_ref[...])


def _out_proj(x, y, w_out):
    b, s, d = x.shape
    t = SEQ_TILE
    e = y.shape[-1]
    return pl.pallas_call(
        _out_proj_kernel,
        grid=(b, s // t),
        in_specs=[
            pl.BlockSpec((1, t, d), lambda bi, si: (bi, si, 0)),
            pl.BlockSpec((1, t, e), lambda bi, si: (bi, si, 0)),
            _resident((e, d)),
        ],
        out_specs=pl.BlockSpec((1, t, d), lambda bi, si: (bi, si, 0)),
        out_shape=jax.ShapeDtypeStruct(x.shape, F32),
        compiler_params=_params(2),
        name="fox_out_proj",
    )(x, y, w_out.astype(BF16))


def _fox_layer(x, g, w_in, f_bias, q_norm, k_norm, w_out):
    qx, kx, vt, gate = _fox_proj(x, g, w_in, f_bias, q_norm, k_norm)
    qk_bound = (1.01 * FOX_HEAD_DIM ** 0.5 * LOG2_E) * jnp.max(jnp.abs(q_norm)) * jnp.max(jnp.abs(k_norm))
    y = lax.cond(qk_bound <= BOUNDED_LOGIT_LOG2,
                 functools.partial(_fox_attention, bounded=True),
                 functools.partial(_fox_attention, bounded=False),
                 qx, kx, vt, gate)
    return _out_proj(x, y, w_out)


def kernel(x, a_norm, a_w_in, a_conv_w, a_conv_b, a_ln_g, a_ln_b, a_w_out, b_norm, b_w_in, b_f_bias, b_q_norm, b_k_norm, b_w_out, c_norm, c_w_in, c_conv_w, c_w_out):
    depth = a_norm.shape[0] + b_norm.shape[0] + c_norm.shape[0]
    for i in range(depth):
        kind, j = i % 3, i // 3
        if kind == 0:
            x = _conformer_layer(x, a_norm[j], a_w_in[j], a_conv_w[j], a_conv_b[j],
                                 a_ln_g[j], a_ln_b[j], a_w_out[j])
        elif kind == 1:
            x = _fox_layer(x, b_norm[j], b_w_in[j], b_f_bias[j], b_q_norm[j], b_k_norm[j],
                           b_w_out[j])
        else:
            x = _short_conv_layer(x, c_norm[j], c_w_in[j], c_conv_w[j], c_w_out[j])
    return x
```

```python
import functools

import jax
import jax.numpy as jnp
from jax import lax
from jax.experimental import pallas as pl
from jax.experimental.pallas import tpu as pltpu

D_MODEL = 1024
D_INNER = 2048
CONF_KERNEL = 31
SHORT_KERNEL = 3
FOX_HEADS = 16
FOX_HEAD_DIM = 128
NORM_EPS = 1e-6

V7X_LANES = 128
V7X_SUBLANES = 8
V7X_VMEM_LIMIT_BYTES = 58 * 1024 * 1024

SEQ_TILE = 512
CHAN_CHUNK = 512
CONF_HALO = 32
SHORT_HALO = 8
CONV_ROWS = 64
KV_TILE = 512
Q_TILE = 2 * KV_TILE
Q_TILE_BOUNDED = 8 * KV_TILE
QK_WIDTH = 2 * FOX_HEAD_DIM
LOG2_E = 1.4426950408889634
BOUNDED_LOGIT_LOG2 = 50.0

BF16 = jnp.bfloat16
F32 = jnp.float32


def _sigmoid(x):
    return 1.0 / (1.0 + jnp.exp(-x))


def _silu(x):
    return x * _sigmoid(x)


def _rms_norm_rows(x, g):
    ms = jnp.mean(x * x, axis=-1, keepdims=True)
    return x * lax.rsqrt(ms + NORM_EPS) * g


def _dot(a, b):
    return jnp.dot(a, b, preferred_element_type=F32)


def _dot_nt(a, b):
    return lax.dot_general(a, b, (((1,), (1,)), ((), ())), preferred_element_type=F32)


def _split3(x):
    hi = x.astype(BF16)
    r = x - hi.astype(F32)
    mid = r.astype(BF16)
    lo = (r - mid.astype(F32)).astype(BF16)
    return hi, mid, lo


def _resident(shape):
    zeros = (0,) * len(shape)
    return pl.BlockSpec(shape, lambda *_: zeros, pipeline_mode=pl.Buffered(1))


def _params(n_axes):
    return pltpu.CompilerParams(
        dimension_semantics=("arbitrary",) * n_axes,
        vmem_limit_bytes=V7X_VMEM_LIMIT_BYTES,
    )


def _conformer_kernel(x_ref, g_ref, w_in_ref, cw_ref, cb_ref, lg_ref, lb_ref, w_out_ref,
                      o_ref, h_ref, u_ref, c_ref):
    si = pl.program_id(1)
    t = SEQ_TILE
    n_chunks = D_INNER // CHAN_CHUNK

    n_slabs = D_INNER // V7X_LANES
    slabs_per_chunk = CHAN_CHUNK // V7X_LANES

    @pl.when(si == 0)
    def _():
        u_ref[:, 0:CONF_HALO, :] = jnp.zeros((n_slabs, CONF_HALO, V7X_LANES), F32)

    x = x_ref[0]
    h_ref[...] = _rms_norm_rows(x, g_ref[...]).astype(BF16)

    for c in range(n_chunks):
        lo = c * CHAN_CHUNK
        val = _dot(h_ref[...], w_in_ref[:, lo:lo + CHAN_CHUNK])
        gate = _dot(h_ref[...], w_in_ref[:, D_INNER + lo:D_INNER + lo + CHAN_CHUNK])
        u = val * _sigmoid(gate)
        for j in range(slabs_per_chunk):
            u_ref[c * slabs_per_chunk + j, CONF_HALO:CONF_HALO + t, :] = (
                u[:, j * V7X_LANES:(j + 1) * V7X_LANES])

    first = CONF_HALO - (CONF_KERNEL - 1)

    def conv_slab(sl, carry):
        for rb in range(t // CONV_ROWS):
            r0 = rb * CONV_ROWS
            acc = jnp.zeros((CONV_ROWS, V7X_LANES), F32)
            for k in range(CONF_KERNEL):
                acc = acc + (cw_ref[sl, k:k + 1, :]
                             * u_ref[sl, r0 + first + k:r0 + first + k + CONV_ROWS, :])
            c_ref[sl, r0:r0 + CONV_ROWS, :] = acc + cb_ref[sl]
        return carry

    lax.fori_loop(0, n_slabs, conv_slab, 0)

    u_ref[:, 0:CONF_HALO, :] = u_ref[:, t:t + CONF_HALO, :]

    row_sum = c_ref[0]
    for sl in range(1, n_slabs):
        row_sum = row_sum + c_ref[sl]
    mu = jnp.sum(row_sum, axis=-1, keepdims=True) * (1.0 / D_INNER)
    sq_sum = jnp.square(c_ref[0] - mu)
    for sl in range(1, n_slabs):
        sq_sum = sq_sum + jnp.square(c_ref[sl] - mu)
    var = jnp.sum(sq_sum, axis=-1, keepdims=True) * (1.0 / D_INNER)
    rstd = lax.rsqrt(var + NORM_EPS)

    acc = x
    for c in range(n_chunks):
        lo = c * CHAN_CHUNK
        z = _dot(h_ref[...], w_in_ref[:, 2 * D_INNER + lo:2 * D_INNER + lo + CHAN_CHUNK])
        conv = jnp.concatenate(
            [c_ref[c * slabs_per_chunk + j] for j in range(slabs_per_chunk)], axis=-1)
        ln = (conv - mu) * rstd * lg_ref[:, lo:lo + CHAN_CHUNK] + lb_ref[:, lo:lo + CHAN_CHUNK]
        y = (_silu(ln) * _silu(z)).astype(BF16)
        acc = acc + _dot(y, w_out_ref[lo:lo + CHAN_CHUNK, :])
    o_ref[0] = acc


def _conformer_layer(x, g, w_in, conv_w, conv_b, ln_g, ln_b, w_out):
    b, s, d = x.shape
    t = SEQ_TILE
    row = lambda n: _resident((1, n))
    n_slabs = D_INNER // V7X_LANES
    cw = conv_w.reshape(CONF_KERNEL, n_slabs, V7X_LANES).transpose(1, 0, 2)
    cb = conv_b.reshape(n_slabs, 1, V7X_LANES)
    return pl.pallas_call(
        _conformer_kernel,
        grid=(b, s // t),
        in_specs=[
            pl.BlockSpec((1, t, d), lambda bi, si: (bi, si, 0)),
            row(d),
            _resident((d, 3 * D_INNER)),
            _resident((n_slabs, CONF_KERNEL, V7X_LANES)),
            _resident((n_slabs, 1, V7X_LANES)),
            row(D_INNER), row(D_INNER),
            _resident((D_INNER, d)),
        ],
        out_specs=pl.BlockSpec((1, t, d), lambda bi, si: (bi, si, 0)),
        out_shape=jax.ShapeDtypeStruct(x.shape, F32),
        scratch_shapes=[
            pltpu.VMEM((t, d), BF16),
            pltpu.VMEM((n_slabs, t + CONF_HALO, V7X_LANES), F32),
            pltpu.VMEM((n_slabs, t, V7X_LANES), F32),
        ],
        compiler_params=_params(2),
        name="conformer_layer",
    )(x, g.reshape(1, d), w_in.astype(BF16), cw, cb,
      ln_g.reshape(1, -1), ln_b.reshape(1, -1), w_out.astype(BF16))


def _short_conv_kernel(x_ref, g_ref, w_in_ref, cw_ref, w_out_ref, o_ref, h_ref, u_ref):
    si = pl.program_id(1)
    t = SEQ_TILE
    n_chunks = D_INNER // CHAN_CHUNK

    @pl.when(si == 0)
    def _():
        u_ref[0:SHORT_HALO, :] = jnp.zeros((SHORT_HALO, D_INNER), F32)

    x = x_ref[0]
    h_ref[...] = _rms_norm_rows(x, g_ref[...]).astype(BF16)

    acc = x
    for c in range(n_chunks):
        lo = c * CHAN_CHUNK
        cols = lambda part: slice(part * D_INNER + lo, part * D_INNER + lo + CHAN_CHUNK)
        u = _dot(h_ref[...], w_in_ref[:, cols(0)])
        c_gate = _dot(h_ref[...], w_in_ref[:, cols(2)])
        u_ref[SHORT_HALO:SHORT_HALO + t, lo:lo + CHAN_CHUNK] = c_gate * u
        conv = jnp.zeros((t, CHAN_CHUNK), F32)
        for k in range(SHORT_KERNEL):
            first = SHORT_HALO - (SHORT_KERNEL - 1) + k
            conv = conv + cw_ref[k:k + 1, lo:lo + CHAN_CHUNK] * u_ref[first:first + t, lo:lo + CHAN_CHUNK]
        b_gate = _dot(h_ref[...], w_in_ref[:, cols(1)])
        z = _dot(h_ref[...], w_in_ref[:, cols(3)])
        y = (b_gate * conv * _silu(z)).astype(BF16)
        acc = acc + _dot(y, w_out_ref[lo:lo + CHAN_CHUNK, :])
    o_ref[0] = acc

    u_ref[0:SHORT_HALO, :] = u_ref[t:t + SHORT_HALO, :]


def _short_conv_layer(x, g, w_in, conv_w, w_out):
    b, s, d = x.shape
    t = SEQ_TILE
    return pl.pallas_call(
        _short_conv_kernel,
        grid=(b, s // t),
        in_specs=[
            pl.BlockSpec((1, t, d), lambda bi, si: (bi, si, 0)),
            _resident((1, d)),
            _resident((d, 4 * D_INNER)),
            _resident((SHORT_KERNEL, D_INNER)),
            _resident((D_INNER, d)),
        ],
        out_specs=pl.BlockSpec((1, t, d), lambda bi, si: (bi, si, 0)),
        out_shape=jax.ShapeDtypeStruct(x.shape, F32),
        scratch_shapes=[
            pltpu.VMEM((t, d), BF16),
            pltpu.VMEM((t + SHORT_HALO, D_INNER), F32),
        ],
        compiler_params=_params(2),
        name="short_conv_layer",
    )(x, g.reshape(1, d), w_in.astype(BF16), conv_w, w_out.astype(BF16))


def _fox_proj_kernel(x_ref, g_ref, wqk_ref, wvt_ref, wz_ref, wf_hi_ref, wf_lo_ref, fb_ref,
                     qg_ref, kg_ref, qx_ref, kx_ref, vt_ref, gate_ref, h_ref, carry_ref):
    si = pl.program_id(1)
    t = SEQ_TILE
    dh = FOX_HEAD_DIM

    @pl.when(si == 0)
    def _():
        carry_ref[...] = jnp.zeros_like(carry_ref)

    x = x_ref[0]
    hf = _rms_norm_rows(x, g_ref[...])
    h_ref[...] = hf.astype(BF16)

    h_hi = h_ref[...]
    h_lo = (hf - h_hi.astype(F32)).astype(BF16)
    f_logit = (_dot(h_hi, wf_hi_ref[...]) + _dot(h_hi, wf_lo_ref[...])
               + _dot(h_lo, wf_hi_ref[...])) + fb_ref[...]
    log_f = jnp.minimum(f_logit, 0.0) - jnp.log1p(jnp.exp(-jnp.abs(f_logit)))
    rows = lax.broadcasted_iota(jnp.int32, (t, t), 0)
    cols = lax.broadcasted_iota(jnp.int32, (t, t), 1)
    tri = jnp.where(cols <= rows, 1.0, 0.0).astype(BF16)
    f_hi, f_mid, f_lo = _split3(log_f)
    c = (_dot(tri, f_hi) + _dot(tri, f_mid) + _dot(tri, f_lo)) + carry_ref[...]
    carry_ref[...] = c[t - 1:t, :]
    c_hi, c_mid, c_lo = _split3(c * LOG2_E)
    c_hi, c_mid, c_lo = c_hi.astype(F32), c_mid.astype(F32), c_lo.astype(F32)

    lane = lax.broadcasted_iota(jnp.int32, (t, dh), 1)
    scale = dh ** -0.5 * LOG2_E
    heads_per_chunk = CHAN_CHUNK // dh
    for c4 in range(D_INNER // CHAN_CHUNK):
        lo = c4 * CHAN_CHUNK
        q4 = _dot(h_ref[...], wqk_ref[:, lo:lo + CHAN_CHUNK])
        k4 = _dot(h_ref[...], wqk_ref[:, D_INNER + lo:D_INNER + lo + CHAN_CHUNK])
        for j in range(heads_per_chunk):
            hd = c4 * heads_per_chunk + j
            qh = _rms_norm_rows(q4[:, j * dh:(j + 1) * dh], qg_ref[...]) * scale
            kh = _rms_norm_rows(k4[:, j * dh:(j + 1) * dh], kg_ref[...])
            hi, mid, lw = c_hi[:, hd:hd + 1], c_mid[:, hd:hd + 1], c_lo[:, hd:hd + 1]
            q_ext = jnp.where(lane < 3, 1.0,
                              jnp.where(lane == 3, hi,
                                        jnp.where(lane == 4, mid,
                                                  jnp.where(lane == 5, lw, 0.0))))
            k_ext = jnp.where(lane == 0, -hi,
                              jnp.where(lane == 1, -mid,
                                        jnp.where(lane == 2, -lw,
                                                  jnp.where(lane < 6, 1.0, 0.0))))
            qx_ref[0, hd, :, 0:dh] = qh.astype(BF16)
            qx_ref[0, hd, :, dh:2 * dh] = q_ext.astype(BF16)
            kx_ref[0, hd, :, 0:dh] = kh.astype(BF16)
            kx_ref[0, hd, :, dh:2 * dh] = k_ext.astype(BF16)
        z = _dot(h_ref[...], wz_ref[:, lo:lo + CHAN_CHUNK])
        gate_ref[0, :, lo:lo + CHAN_CHUNK] = _silu(z).astype(BF16)
        vt = _dot_nt(wvt_ref[lo:lo + CHAN_CHUNK, :], h_ref[...])
        for j in range(heads_per_chunk):
            hd = c4 * heads_per_chunk + j
            vt_ref[0, hd, 0] = vt[j * dh:(j + 1) * dh, :].astype(BF16)


def _fox_proj(x, g, w_in, f_bias, q_norm, k_norm):
    b, s, d = x.shape
    t = SEQ_TILE
    h, dh = FOX_HEADS, FOX_HEAD_DIM
    e = D_INNER
    wqk = w_in[:, :2 * e].astype(BF16)
    wvt = w_in[:, 2 * e:3 * e].T.astype(BF16)
    wz = w_in[:, 3 * e:4 * e].astype(BF16)
    wf = w_in[:, 4 * e:]
    wf_hi = wf.astype(BF16)
    wf_lo = (wf - wf_hi.astype(F32)).astype(BF16)
    return pl.pallas_call(
        _fox_proj_kernel,
        grid=(b, s // t),
        in_specs=[
            pl.BlockSpec((1, t, d), lambda bi, si: (bi, si, 0)),
            _resident((1, d)),
            _resident((d, 2 * e)),
            _resident((e, d)),
            _resident((d, e)),
            _resident((d, h)), _resident((d, h)), _resident((1, h)),
            _resident((1, dh)), _resident((1, dh)),
        ],
        out_specs=[
            pl.BlockSpec((1, h, t, QK_WIDTH), lambda bi, si: (bi, 0, si, 0)),
            pl.BlockSpec((1, h, t, QK_WIDTH), lambda bi, si: (bi, 0, si, 0)),
            pl.BlockSpec((1, h, 1, dh, t), lambda bi, si: (bi, 0, si, 0, 0)),
            pl.BlockSpec((1, t, e), lambda bi, si: (bi, si, 0)),
        ],
        out_shape=[
            jax.ShapeDtypeStruct((b, h, s, QK_WIDTH), BF16),
            jax.ShapeDtypeStruct((b, h, s, QK_WIDTH), BF16),
            jax.ShapeDtypeStruct((b, h, s // t, dh, t), BF16),
            jax.ShapeDtypeStruct((b, s, e), BF16),
        ],
        scratch_shapes=[
            pltpu.VMEM((t, d), BF16),
            pltpu.VMEM((1, h), F32),
        ],
        compiler_params=_params(2),
        name="fox_proj",
    )(x, g.reshape(1, d), wqk, wvt, wz, wf_hi, wf_lo, f_bias.reshape(1, h),
      q_norm.reshape(1, dh), k_norm.reshape(1, dh))


def _fox_attn_kernel(qx_ref, kx_ref, vt_ref, gate_ref, y_ref,
                     s0_ref, s1_ref, p0_ref, p1_ref, a0_ref, a1_ref, m_ref, l_ref, acc_ref):
    qi = pl.program_id(2)
    kt = KV_TILE

    def scores(blk, s_ref, c0, c1):
        k0 = pl.multiple_of(blk * kt, kt)
        s_ref[:, c0:c1] = _dot_nt(kx_ref[0, 0, pl.ds(k0, kt), :], qx_ref[0, 0, c0:c1, :])

    def softmax(s_ref, p_ref, a_ref, c0, c1, masked):
        s = s_ref[:, c0:c1]
        if masked:
            kv_pos = lax.broadcasted_iota(jnp.int32, s.shape, 0)
            q_pos = lax.broadcasted_iota(jnp.int32, s.shape, 1)
            s = jnp.where(kv_pos <= q_pos, s, -jnp.inf)
        m_prev = m_ref[:, c0:c1]
        m_new = jnp.maximum(m_prev, jnp.max(s, axis=0, keepdims=True))
        p = jnp.exp2(s - m_new)
        alpha = jnp.exp2(m_prev - m_new)
        l_ref[:, c0:c1] = alpha * l_ref[:, c0:c1] + jnp.sum(p, axis=0, keepdims=True)
        m_ref[:, c0:c1] = m_new
        a_ref[:, c0:c1] = alpha
        p_ref[:, c0:c1] = p.astype(BF16)

    def accumulate(blk, p_ref, a_ref, c0, c1):
        acc_ref[:, c0:c1] = (a_ref[:, c0:c1] * acc_ref[:, c0:c1]
                             + _dot(vt_ref[0, 0, blk], p_ref[:, c0:c1]))

    m_ref[...] = jnp.full(m_ref.shape, -1e30, F32)
    l_ref[...] = jnp.zeros(l_ref.shape, F32)
    acc_ref[...] = jnp.zeros(acc_ref.shape, F32)
    p1_ref[...] = jnp.zeros(p1_ref.shape, BF16)
    a1_ref[...] = jnp.ones(a1_ref.shape, F32)

    full = (0, Q_TILE)
    scores(0, s0_ref, *full)

    def pair(t, carry):
        even = 2 * t
        scores(even + 1, s1_ref, *full)
        softmax(s0_ref, p0_ref, a0_ref, *full, masked=False)
        accumulate(jnp.maximum(even - 1, 0), p1_ref, a1_ref, *full)
        scores(even + 2, s0_ref, *full)
        softmax(s1_ref, p1_ref, a1_ref, *full, masked=False)
        accumulate(even, p0_ref, a0_ref, *full)
        return carry

    lax.fori_loop(0, qi, pair, 0)

    d0 = 2 * qi
    left, right = (0, kt), (kt, Q_TILE)
    scores(d0 + 1, s1_ref, *right)
    softmax(s0_ref, p0_ref, a0_ref, *left, masked=True)
    softmax(s0_ref, p0_ref, a0_ref, *right, masked=False)
    accumulate(jnp.maximum(d0 - 1, 0), p1_ref, a1_ref, *full)
    softmax(s1_ref, p1_ref, a1_ref, *right, masked=True)
    accumulate(d0, p0_ref, a0_ref, *full)
    accumulate(d0 + 1, p1_ref, a1_ref, *right)

    o_t = acc_ref[...] * (1.0 / l_ref[...])
    y_ref[0] = (o_t.T * gate_ref[0].astype(F32)).astype(BF16)


def _fox_attn_bounded_kernel(qx_ref, kx_ref, vt_ref, gate_ref, y_ref,
                             p0_ref, p1_ref, l_ref, acc_ref):
    qi = pl.program_id(2)
    kt = KV_TILE

    def probs(blk, p_ref, c0, c1, masked):
        k0 = pl.multiple_of(blk * kt, kt)
        s = _dot_nt(kx_ref[0, 0, pl.ds(k0, kt), :], qx_ref[0, 0, c0:c1, :])
        if masked:
            kv_pos = lax.broadcasted_iota(jnp.int32, s.shape, 0)
            q_pos = lax.broadcasted_iota(jnp.int32, s.shape, 1)
            s = jnp.where(kv_pos <= q_pos, s, -jnp.inf)
        p = jnp.exp2(s)
        l_ref[:, c0:c1] = l_ref[:, c0:c1] + jnp.sum(p, axis=0, keepdims=True)
        p_ref[:, c0:c1] = p.astype(BF16)

    def accumulate(blk, p_ref, c0, c1):
        acc_ref[:, c0:c1] = acc_ref[:, c0:c1] + _dot(vt_ref[0, 0, blk], p_ref[:, c0:c1])

    l_ref[...] = jnp.zeros(l_ref.shape, F32)
    acc_ref[...] = jnp.zeros(acc_ref.shape, F32)
    p1_ref[...] = jnp.zeros(p1_ref.shape, BF16)

    q_tile = p0_ref.shape[1]
    n_diag = q_tile // kt
    p_refs = (p0_ref, p1_ref)
    full = (0, q_tile)

    def pair(t, carry):
        even = 2 * t
        probs(even, p0_ref, *full, masked=False)
        accumulate(jnp.maximum(even - 1, 0), p1_ref, *full)
        probs(even + 1, p1_ref, *full, masked=False)
        accumulate(even, p0_ref, *full)
        return carry

    d0 = n_diag * qi
    lax.fori_loop(0, d0 // 2, pair, 0)

    def probs_diag(i):
        probs(d0 + i, p_refs[i % 2], i * kt, (i + 1) * kt, masked=True)
        if i + 1 < n_diag:
            probs(d0 + i, p_refs[i % 2], (i + 1) * kt, q_tile, masked=False)

    probs_diag(0)
    accumulate(jnp.maximum(d0 - 1, 0), p1_ref, *full)
    for i in range(1, n_diag):
        probs_diag(i)
        accumulate(d0 + i - 1, p_refs[(i - 1) % 2], (i - 1) * kt, q_tile)
    accumulate(d0 + n_diag - 1, p_refs[(n_diag - 1) % 2], (n_diag - 1) * kt, q_tile)

    o_t = acc_ref[...] * (1.0 / l_ref[...])
    y_ref[0] = (o_t.T * gate_ref[0].astype(F32)).astype(BF16)


def _fox_attention(qx, kx, vt, gate, *, bounded):
    b, h, s, _ = qx.shape
    dh = FOX_HEAD_DIM
    assert KV_TILE == SEQ_TILE and Q_TILE == 2 * KV_TILE and Q_TILE_BOUNDED % (2 * KV_TILE) == 0
    q_tile = Q_TILE_BOUNDED if bounded else Q_TILE
    probs = pltpu.VMEM((KV_TILE, q_tile), BF16)
    row = pltpu.VMEM((1, q_tile), F32)
    acc = pltpu.VMEM((dh, q_tile), F32)
    if bounded:
        body, scratch = _fox_attn_bounded_kernel, [probs, probs, row, acc]
    else:
        scores = pltpu.VMEM((KV_TILE, q_tile), F32)
        body, scratch = _fox_attn_kernel, [scores, scores, probs, probs, row, row, row, row, acc]
    return pl.pallas_call(
        body,
        grid=(b, h, s // q_tile),
        in_specs=[
            pl.BlockSpec((1, 1, q_tile, QK_WIDTH), lambda bi, hi, qi: (bi, hi, qi, 0)),
            pl.BlockSpec((1, 1, s, QK_WIDTH), lambda bi, hi, qi: (bi, hi, 0, 0)),
            pl.BlockSpec((1, 1, s // KV_TILE, dh, KV_TILE), lambda bi, hi, qi: (bi, hi, 0, 0, 0)),
            pl.BlockSpec((1, q_tile, dh), lambda bi, hi, qi: (bi, qi, hi)),
        ],
        out_specs=pl.BlockSpec((1, q_tile, dh), lambda bi, hi, qi: (bi, qi, hi)),
        out_shape=jax.ShapeDtypeStruct((b, s, h * dh), BF16),
        scratch_shapes=scratch,
        compiler_params=_params(3),
        name="fox_attention_bounded" if bounded else "fox_attention",
    )(qx, kx, vt, gate)


def _out_proj_kernel(x_ref, y_ref, w_ref, o_ref):
    o_ref[0] = x_ref[0] + _dot(y_ref[0], w_ref[...])


def _out_proj(x, y, w_out):
    b, s, d = x.shape
    t = SEQ_TILE
    e = y.shape[-1]
    return pl.pallas_call(
        _out_proj_kernel,
        grid=(b, s // t),
        in_specs=[
            pl.BlockSpec((1, t, d), lambda bi, si: (bi, si, 0)),
            pl.BlockSpec((1, t, e), lambda bi, si: (bi, si, 0)),
            _resident((e, d)),
        ],
        out_specs=pl.BlockSpec((1, t, d), lambda bi, si: (bi, si, 0)),
        out_shape=jax.ShapeDtypeStruct(x.shape, F32),
        compiler_params=_params(2),
        name="fox_out_proj",
    )(x, y, w_out.astype(BF16))


def _fox_layer(x, g, w_in, f_bias, q_norm, k_norm, w_out):
    qx, kx, vt, gate = _fox_proj(x, g, w_in, f_bias, q_norm, k_norm)
    qk_bound = (1.01 * FOX_HEAD_DIM ** 0.5 * LOG2_E) * jnp.max(jnp.abs(q_norm)) * jnp.max(jnp.abs(k_norm))
    y = lax.cond(qk_bound <= BOUNDED_LOGIT_LOG2,
                 functools.partial(_fox_attention, bounded=True),
                 functools.partial(_fox_attention, bounded=False),
                 qx, kx, vt, gate)
    return _out_proj(x, y, w_out)


def kernel(x, a_norm, a_w_in, a_conv_w, a_conv_b, a_ln_g, a_ln_b, a_w_out, b_norm, b_w_in, b_f_bias, b_q_norm, b_k_norm, b_w_out, c_norm, c_w_in, c_conv_w, c_w_out):
    depth = a_norm.shape[0] + b_norm.shape[0] + c_norm.shape[0]
    for i in range(depth):
        kind, j = i % 3, i // 3
        if kind == 0:
            x = _conformer_layer(x, a_norm[j], a_w_in[j], a_conv_w[j], a_conv_b[j],
                                 a_ln_g[j], a_ln_b[j], a_w_out[j])
        elif kind == 1:
            x = _fox_layer(x, b_norm[j], b_w_in[j], b_f_bias[j], b_q_norm[j], b_k_norm[j],
                           b_w_out[j])
        else:
            x = _short_conv_layer(x, c_norm[j], c_w_in[j], c_conv_w[j], c_w_out[j])
    return x
```

```python
import functools

import jax
import jax.numpy as jnp
from jax import lax
from jax.experimental import pallas as pl
from jax.experimental.pallas import tpu as pltpu

D_MODEL = 1024
D_INNER = 2048
CONF_KERNEL = 31
SHORT_KERNEL = 3
FOX_HEADS = 16
FOX_HEAD_DIM = 128
NORM_EPS = 1e-6

V7X_LANES = 128
V7X_SUBLANES = 8
V7X_VMEM_LIMIT_BYTES = 58 * 1024 * 1024

SEQ_TILE = 512
CHAN_CHUNK = 512
CONF_HALO = 32
SHORT_HALO = 8
SLABS_PER_Z = 2
PROJ_ROWS = 128
KV_TILE = 512
Q_TILE = 2 * KV_TILE
Q_TILE_BOUNDED = 8 * KV_TILE
QK_WIDTH = 2 * FOX_HEAD_DIM
LOG2_E = 1.4426950408889634
BOUNDED_LOGIT_LOG2 = 50.0

BF16 = jnp.bfloat16
F32 = jnp.float32


def _sigmoid(x):
    return 1.0 / (1.0 + jnp.exp(-x))


def _silu(x):
    return x * _sigmoid(x)


def _rms_norm_rows(x, g):
    ms = jnp.mean(x * x, axis=-1, keepdims=True)
    return x * lax.rsqrt(ms + NORM_EPS) * g


def _dot(a, b):
    return jnp.dot(a, b, preferred_element_type=F32)


def _dot_nt(a, b):
    return lax.dot_general(a, b, (((1,), (1,)), ((), ())), preferred_element_type=F32)


def _split3(x):
    hi = x.astype(BF16)
    r = x - hi.astype(F32)
    mid = r.astype(BF16)
    lo = (r - mid.astype(F32)).astype(BF16)
    return hi, mid, lo


def _resident(shape):
    zeros = (0,) * len(shape)
    return pl.BlockSpec(shape, lambda *_: zeros, pipeline_mode=pl.Buffered(1))


def _params(n_axes):
    return pltpu.CompilerParams(
        dimension_semantics=("arbitrary",) * n_axes,
        vmem_limit_bytes=V7X_VMEM_LIMIT_BYTES,
    )


def _conformer_kernel(x_ref, g_ref, w_in_ref, cw_ref, cb_ref, lg_ref, lb_ref, w_out_ref,
                      o_ref, h_ref, u_ref, c_ref, zg_ref):
    si = pl.program_id(1)
    t = SEQ_TILE
    n_chunks = D_INNER // CHAN_CHUNK

    n_slabs = D_INNER // V7X_LANES
    slabs_per_chunk = CHAN_CHUNK // V7X_LANES

    @pl.when(si == 0)
    def _():
        u_ref[:, 0:CONF_HALO, :] = jnp.zeros((n_slabs, CONF_HALO, V7X_LANES), F32)

    x = x_ref[0]
    h_ref[...] = _rms_norm_rows(x, g_ref[...]).astype(BF16)

    n_z = n_slabs // SLABS_PER_Z

    row_pieces = [slice(q * PROJ_ROWS, (q + 1) * PROJ_ROWS) for q in range(t // PROJ_ROWS)]

    def project_steps(i):
        def piece(r):
            val = _dot(h_ref[r, :], w_in_ref[i])
            gate = _dot(h_ref[r, :], w_in_ref[n_z + i])
            u = val * _sigmoid(gate)
            for j in range(SLABS_PER_Z):
                u_ref[i * SLABS_PER_Z + j, CONF_HALO + r.start:CONF_HALO + r.stop, :] = (
                    u[:, j * V7X_LANES:(j + 1) * V7X_LANES])
        return [functools.partial(piece, r) for r in row_pieces]

    def gate_steps(i):
        def piece(r):
            zg_ref[i, r, :] = _silu(_dot(h_ref[r, :], w_in_ref[2 * n_z + i]))
        return [functools.partial(piece, r) for r in row_pieces]

    first = CONF_HALO - (CONF_KERNEL - 1)
    rows = V7X_SUBLANES
    n_blocks = t // rows

    def conv_slab(sl, fillers):
        bias = jnp.broadcast_to(cb_ref[sl], (rows, V7X_LANES))
        taps = [jnp.broadcast_to(cw_ref[sl, k:k + 1, :], (rows, V7X_LANES))
                for k in range(CONF_KERNEL)]
        offsets = range(first, first + rows * (n_blocks - 1) + CONF_KERNEL)
        stride = -(-len(offsets) // (len(fillers) + 1))
        acc = {}
        for n, j in enumerate(offsets):
            if n % stride == stride - 1 and fillers:
                fillers.pop(0)()
            window = u_ref[sl, j:j + rows, :]
            lo_b = max(0, -(-(j - first - (CONF_KERNEL - 1)) // rows))
            hi_b = min(n_blocks - 1, (j - first) // rows)
            for b in range(lo_b, hi_b + 1):
                k = j - first - rows * b
                acc[b] = (bias if k == 0 else acc[b]) + taps[k] * window
                if k == CONF_KERNEL - 1:
                    c_ref[sl, rows * b:rows * (b + 1), :] = acc.pop(b)
        for filler in fillers:
            filler()
        u_ref[sl, 0:CONF_HALO, :] = u_ref[sl, t:t + CONF_HALO, :]

    def conv_pair(i, fillers):
        per_slab = -(-len(fillers) // SLABS_PER_Z)
        for j in range(SLABS_PER_Z):
            conv_slab(i * SLABS_PER_Z + j, fillers[j * per_slab:(j + 1) * per_slab])

    for step in project_steps(0):
        step()

    def trip(i, carry):
        conv_pair(i, gate_steps(i) + project_steps(i + 1))
        return carry

    lax.fori_loop(0, n_z - 1, trip, 0)
    conv_pair(n_z - 1, gate_steps(n_z - 1))

    row_sum = c_ref[0]
    for sl in range(1, n_slabs):
        row_sum = row_sum + c_ref[sl]
    mu = jnp.sum(row_sum, axis=-1, keepdims=True) * (1.0 / D_INNER)
    sq_sum = jnp.square(c_ref[0] - mu)
    for sl in range(1, n_slabs):
        sq_sum = sq_sum + jnp.square(c_ref[sl] - mu)
    var = jnp.sum(sq_sum, axis=-1, keepdims=True) * (1.0 / D_INNER)
    rstd = lax.rsqrt(var + NORM_EPS)

    z_per_chunk = slabs_per_chunk // SLABS_PER_Z
    acc = x
    for c in range(n_chunks):
        lo = c * CHAN_CHUNK
        conv = jnp.concatenate(
            [c_ref[c * slabs_per_chunk + j] for j in range(slabs_per_chunk)], axis=-1)
        z_gate = jnp.concatenate(
            [zg_ref[c * z_per_chunk + j] for j in range(z_per_chunk)], axis=-1)
        ln = (conv - mu) * rstd * lg_ref[:, lo:lo + CHAN_CHUNK] + lb_ref[:, lo:lo + CHAN_CHUNK]
        y = (_silu(ln) * z_gate).astype(BF16)
        acc = acc + _dot(y, w_out_ref[lo:lo + CHAN_CHUNK, :])
    o_ref[0] = acc


def _conformer_layer(x, g, w_in, conv_w, conv_b, ln_g, ln_b, w_out):
    b, s, d = x.shape
    t = SEQ_TILE
    row = lambda n: _resident((1, n))
    n_slabs = D_INNER // V7X_LANES
    cw = conv_w.reshape(CONF_KERNEL, n_slabs, V7X_LANES).transpose(1, 0, 2)
    cb = conv_b.reshape(n_slabs, 1, V7X_LANES)
    z_width = SLABS_PER_Z * V7X_LANES
    n_z = D_INNER // z_width
    w3 = (w_in.reshape(d, 3, n_z, z_width).transpose(1, 2, 0, 3)
          .reshape(3 * n_z, d, z_width).astype(BF16))
    return pl.pallas_call(
        _conformer_kernel,
        grid=(b, s // t),
        in_specs=[
            pl.BlockSpec((1, t, d), lambda bi, si: (bi, si, 0)),
            row(d),
            _resident((3 * n_z, d, z_width)),
            _resident((n_slabs, CONF_KERNEL, V7X_LANES)),
            _resident((n_slabs, 1, V7X_LANES)),
            row(D_INNER), row(D_INNER),
            _resident((D_INNER, d)),
        ],
        out_specs=pl.BlockSpec((1, t, d), lambda bi, si: (bi, si, 0)),
        out_shape=jax.ShapeDtypeStruct(x.shape, F32),
        scratch_shapes=[
            pltpu.VMEM((t, d), BF16),
            pltpu.VMEM((n_slabs, t + CONF_HALO, V7X_LANES), F32),
            pltpu.VMEM((n_slabs, t, V7X_LANES), F32),
            pltpu.VMEM((n_z, t, z_width), F32),
        ],
        compiler_params=_params(2),
        name="conformer_layer",
    )(x, g.reshape(1, d), w3, cw, cb,
      ln_g.reshape(1, -1), ln_b.reshape(1, -1), w_out.astype(BF16))


def _short_conv_kernel(x_ref, g_ref, w_in_ref, cw_ref, w_out_ref, o_ref, h_ref, u_ref):
    si = pl.program_id(1)
    t = SEQ_TILE
    n_chunks = D_INNER // CHAN_CHUNK

    @pl.when(si == 0)
    def _():
        u_ref[0:SHORT_HALO, :] = jnp.zeros((SHORT_HALO, D_INNER), F32)

    x = x_ref[0]
    h_ref[...] = _rms_norm_rows(x, g_ref[...]).astype(BF16)

    acc = x
    for c in range(n_chunks):
        lo = c * CHAN_CHUNK
        cols = lambda part: slice(part * D_INNER + lo, part * D_INNER + lo + CHAN_CHUNK)
        u = _dot(h_ref[...], w_in_ref[:, cols(0)])
        c_gate = _dot(h_ref[...], w_in_ref[:, cols(2)])
        u_ref[SHORT_HALO:SHORT_HALO + t, lo:lo + CHAN_CHUNK] = c_gate * u
        conv = jnp.zeros((t, CHAN_CHUNK), F32)
        for k in range(SHORT_KERNEL):
            first = SHORT_HALO - (SHORT_KERNEL - 1) + k
            conv = conv + cw_ref[k:k + 1, lo:lo + CHAN_CHUNK] * u_ref[first:first + t, lo:lo + CHAN_CHUNK]
        b_gate = _dot(h_ref[...], w_in_ref[:, cols(1)])
        z = _dot(h_ref[...], w_in_ref[:, cols(3)])
        y = (b_gate * conv * _silu(z)).astype(BF16)
        acc = acc + _dot(y, w_out_ref[lo:lo + CHAN_CHUNK, :])
    o_ref[0] = acc

    u_ref[0:SHORT_HALO, :] = u_ref[t:t + SHORT_HALO, :]


def _short_conv_layer(x, g, w_in, conv_w, w_out):
    b, s, d = x.shape
    t = SEQ_TILE
    return pl.pallas_call(
        _short_conv_kernel,
        grid=(b, s // t),
        in_specs=[
            pl.BlockSpec((1, t, d), lambda bi, si: (bi, si, 0)),
            _resident((1, d)),
            _resident((d, 4 * D_INNER)),
            _resident((SHORT_KERNEL, D_INNER)),
            _resident((D_INNER, d)),
        ],
        out_specs=pl.BlockSpec((1, t, d), lambda bi, si: (bi, si, 0)),
        out_shape=jax.ShapeDtypeStruct(x.shape, F32),
        scratch_shapes=[
            pltpu.VMEM((t, d), BF16),
            pltpu.VMEM((t + SHORT_HALO, D_INNER), F32),
        ],
        compiler_params=_params(2),
        name="short_conv_layer",
    )(x, g.reshape(1, d), w_in.astype(BF16), conv_w, w_out.astype(BF16))


def _fox_proj_kernel(x_ref, g_ref, wqk_ref, wvt_ref, wz_ref, wf_hi_ref, wf_lo_ref, fb_ref,
                     qg_ref, kg_ref, qx_ref, kx_ref, vt_ref, gate_ref, h_ref, carry_ref):
    si = pl.program_id(1)
    t = SEQ_TILE
    dh = FOX_HEAD_DIM

    @pl.when(si == 0)
    def _():
        carry_ref[...] = jnp.zeros_like(carry_ref)

    x = x_ref[0]
    hf = _rms_norm_rows(x, g_ref[...])
    h_ref[...] = hf.astype(BF16)

    h_hi = h_ref[...]
    h_lo = (hf - h_hi.astype(F32)).astype(BF16)
    f_logit = (_dot(h_hi, wf_hi_ref[...]) + _dot(h_hi, wf_lo_ref[...])
               + _dot(h_lo, wf_hi_ref[...])) + fb_ref[...]
    log_f = jnp.minimum(f_logit, 0.0) - jnp.log1p(jnp.exp(-jnp.abs(f_logit)))
    rows = lax.broadcasted_iota(jnp.int32, (t, t), 0)
    cols = lax.broadcasted_iota(jnp.int32, (t, t), 1)
    tri = jnp.where(cols <= rows, 1.0, 0.0).astype(BF16)
    f_hi, f_mid, f_lo = _split3(log_f)
    c = (_dot(tri, f_hi) + _dot(tri, f_mid) + _dot(tri, f_lo)) + carry_ref[...]
    carry_ref[...] = c[t - 1:t, :]
    c_hi, c_mid, c_lo = _split3(c * LOG2_E)
    c_hi, c_mid, c_lo = c_hi.astype(F32), c_mid.astype(F32), c_lo.astype(F32)

    lane = lax.broadcasted_iota(jnp.int32, (t, dh), 1)
    scale = dh ** -0.5 * LOG2_E
    heads_per_chunk = CHAN_CHUNK // dh
    for c4 in range(D_INNER // CHAN_CHUNK):
        lo = c4 * CHAN_CHUNK
        q4 = _dot(h_ref[...], wqk_ref[:, lo:lo + CHAN_CHUNK])
        k4 = _dot(h_ref[...], wqk_ref[:, D_INNER + lo:D_INNER + lo + CHAN_CHUNK])
        for j in range(heads_per_chunk):
            hd = c4 * heads_per_chunk + j
            qh = _rms_norm_rows(q4[:, j * dh:(j + 1) * dh], qg_ref[...]) * scale
            kh = _rms_norm_rows(k4[:, j * dh:(j + 1) * dh], kg_ref[...])
            hi, mid, lw = c_hi[:, hd:hd + 1], c_mid[:, hd:hd + 1], c_lo[:, hd:hd + 1]
            q_ext = jnp.where(lane < 3, 1.0,
                              jnp.where(lane == 3, hi,
                                        jnp.where(lane == 4, mid,
                                                  jnp.where(lane == 5, lw, 0.0))))
            k_ext = jnp.where(lane == 0, -hi,
                              jnp.where(lane == 1, -mid,
                                        jnp.where(lane == 2, -lw,
                                                  jnp.where(lane < 6, 1.0, 0.0))))
            qx_ref[0, hd, :, 0:dh] = qh.astype(BF16)
            qx_ref[0, hd, :, dh:2 * dh] = q_ext.astype(BF16)
            kx_ref[0, hd, :, 0:dh] = kh.astype(BF16)
            kx_ref[0, hd, :, dh:2 * dh] = k_ext.astype(BF16)
        z = _dot(h_ref[...], wz_ref[:, lo:lo + CHAN_CHUNK])
        gate_ref[0, :, lo:lo + CHAN_CHUNK] = _silu(z).astype(BF16)
        vt = _dot_nt(wvt_ref[lo:lo + CHAN_CHUNK, :], h_ref[...])
        for j in range(heads_per_chunk):
            hd = c4 * heads_per_chunk + j
            vt_ref[0, hd, 0] = vt[j * dh:(j + 1) * dh, :].astype(BF16)


def _fox_proj(x, g, w_in, f_bias, q_norm, k_norm):
    b, s, d = x.shape
    t = SEQ_TILE
    h, dh = FOX_HEADS, FOX_HEAD_DIM
    e = D_INNER
    wqk = w_in[:, :2 * e].astype(BF16)
    wvt = w_in[:, 2 * e:3 * e].T.astype(BF16)
    wz = w_in[:, 3 * e:4 * e].astype(BF16)
    wf = w_in[:, 4 * e:]
    wf_hi = wf.astype(BF16)
    wf_lo = (wf - wf_hi.astype(F32)).astype(BF16)
    return pl.pallas_call(
        _fox_proj_kernel,
        grid=(b, s // t),
        in_specs=[
            pl.BlockSpec((1, t, d), lambda bi, si: (bi, si, 0)),
            _resident((1, d)),
            _resident((d, 2 * e)),
            _resident((e, d)),
            _resident((d, e)),
            _resident((d, h)), _resident((d, h)), _resident((1, h)),
            _resident((1, dh)), _resident((1, dh)),
        ],
        out_specs=[
            pl.BlockSpec((1, h, t, QK_WIDTH), lambda bi, si: (bi, 0, si, 0)),
            pl.BlockSpec((1, h, t, QK_WIDTH), lambda bi, si: (bi, 0, si, 0)),
            pl.BlockSpec((1, h, 1, dh, t), lambda bi, si: (bi, 0, si, 0, 0)),
            pl.BlockSpec((1, t, e), lambda bi, si: (bi, si, 0)),
        ],
        out_shape=[
            jax.ShapeDtypeStruct((b, h, s, QK_WIDTH), BF16),
            jax.ShapeDtypeStruct((b, h, s, QK_WIDTH), BF16),
            jax.ShapeDtypeStruct((b, h, s // t, dh, t), BF16),
            jax.ShapeDtypeStruct((b, s, e), BF16),
        ],
        scratch_shapes=[
            pltpu.VMEM((t, d), BF16),
            pltpu.VMEM((1, h), F32),
        ],
        compiler_params=_params(2),
        name="fox_proj",
    )(x, g.reshape(1, d), wqk, wvt, wz, wf_hi, wf_lo, f_bias.reshape(1, h),
      q_norm.reshape(1, dh), k_norm.reshape(1, dh))


def _fox_attn_kernel(qx_ref, kx_ref, vt_ref, gate_ref, y_ref,
                     s0_ref, s1_ref, p0_ref, p1_ref, a0_ref, a1_ref, m_ref, l_ref, acc_ref):
    qi = pl.program_id(2)
    kt = KV_TILE

    def scores(blk, s_ref, c0, c1):
        k0 = pl.multiple_of(blk * kt, kt)
        s_ref[:, c0:c1] = _dot_nt(kx_ref[0, 0, pl.ds(k0, kt), :], qx_ref[0, 0, c0:c1, :])

    def softmax(s_ref, p_ref, a_ref, c0, c1, masked):
        s = s_ref[:, c0:c1]
        if masked:
            kv_pos = lax.broadcasted_iota(jnp.int32, s.shape, 0)
            q_pos = lax.broadcasted_iota(jnp.int32, s.shape, 1)
            s = jnp.where(kv_pos <= q_pos, s, -jnp.inf)
        m_prev = m_ref[:, c0:c1]
        m_new = jnp.maximum(m_prev, jnp.max(s, axis=0, keepdims=True))
        p = jnp.exp2(s - m_new)
        alpha = jnp.exp2(m_prev - m_new)
        l_ref[:, c0:c1] = alpha * l_ref[:, c0:c1] + jnp.sum(p, axis=0, keepdims=True)
        m_ref[:, c0:c1] = m_new
        a_ref[:, c0:c1] = alpha
        p_ref[:, c0:c1] = p.astype(BF16)

    def accumulate(blk, p_ref, a_ref, c0, c1):
        acc_ref[:, c0:c1] = (a_ref[:, c0:c1] * acc_ref[:, c0:c1]
                             + _dot(vt_ref[0, 0, blk], p_ref[:, c0:c1]))

    m_ref[...] = jnp.full(m_ref.shape, -1e30, F32)
    l_ref[...] = jnp.zeros(l_ref.shape, F32)
    acc_ref[...] = jnp.zeros(acc_ref.shape, F32)
    p1_ref[...] = jnp.zeros(p1_ref.shape, BF16)
    a1_ref[...] = jnp.ones(a1_ref.shape, F32)

    full = (0, Q_TILE)
    scores(0, s0_ref, *full)

    def pair(t, carry):
        even = 2 * t
        scores(even + 1, s1_ref, *full)
        softmax(s0_ref, p0_ref, a0_ref, *full, masked=False)
        accumulate(jnp.maximum(even - 1, 0), p1_ref, a1_ref, *full)
        scores(even + 2, s0_ref, *full)
        softmax(s1_ref, p1_ref, a1_ref, *full, masked=False)
        accumulate(even, p0_ref, a0_ref, *full)
        return carry

    lax.fori_loop(0, qi, pair, 0)

    d0 = 2 * qi
    left, right = (0, kt), (kt, Q_TILE)
    scores(d0 + 1, s1_ref, *right)
    softmax(s0_ref, p0_ref, a0_ref, *left, masked=True)
    softmax(s0_ref, p0_ref, a0_ref, *right, masked=False)
    accumulate(jnp.maximum(d0 - 1, 0), p1_ref, a1_ref, *full)
    softmax(s1_ref, p1_ref, a1_ref, *right, masked=True)
    accumulate(d0, p0_ref, a0_ref, *full)
    accumulate(d0 + 1, p1_ref, a1_ref, *right)

    o_t = acc_ref[...] * (1.0 / l_ref[...])
    y_ref[0] = (o_t.T * gate_ref[0].astype(F32)).astype(BF16)


def _fox_attn_bounded_kernel(qx_ref, kx_ref, vt_ref, gate_ref, y_ref,
                             p0_ref, p1_ref, l_ref, acc_ref):
    qi = pl.program_id(2)
    kt = KV_TILE

    def probs(blk, p_ref, c0, c1, masked):
        k0 = pl.multiple_of(blk * kt, kt)
        s = _dot_nt(kx_ref[0, 0, pl.ds(k0, kt), :], qx_ref[0, 0, c0:c1, :])
        if masked:
            kv_pos = lax.broadcasted_iota(jnp.int32, s.shape, 0)
            q_pos = lax.broadcasted_iota(jnp.int32, s.shape, 1)
            s = jnp.where(kv_pos <= q_pos, s, -jnp.inf)
        p = jnp.exp2(s)
        l_ref[:, c0:c1] = l_ref[:, c0:c1] + jnp.sum(p, axis=0, keepdims=True)
        p_ref[:, c0:c1] = p.astype(BF16)

    def accumulate(blk, p_ref, c0, c1):
        acc_ref[:, c0:c1] = acc_ref[:, c0:c1] + _dot(vt_ref[0, 0, blk], p_ref[:, c0:c1])

    l_ref[...] = jnp.zeros(l_ref.shape, F32)
    acc_ref[...] = jnp.zeros(acc_ref.shape, F32)
    p1_ref[...] = jnp.zeros(p1_ref.shape, BF16)

    q_tile = p0_ref.shape[1]
    n_diag = q_tile // kt
    p_refs = (p0_ref, p1_ref)
    full = (0, q_tile)

    def pair(t, carry):
        even = 2 * t
        probs(even, p0_ref, *full, masked=False)
        accumulate(jnp.maximum(even - 1, 0), p1_ref, *full)
        probs(even + 1, p1_ref, *full, masked=False)
        accumulate(even, p0_ref, *full)
        return carry

    d0 = n_diag * qi
    lax.fori_loop(0, d0 // 2, pair, 0)

    def probs_diag(i):
        probs(d0 + i, p_refs[i % 2], i * kt, (i + 1) * kt, masked=True)
        if i + 1 < n_diag:
            probs(d0 + i, p_refs[i % 2], (i + 1) * kt, q_tile, masked=False)

    probs_diag(0)
    accumulate(jnp.maximum(d0 - 1, 0), p1_ref, *full)
    for i in range(1, n_diag):
        probs_diag(i)
        accumulate(d0 + i - 1, p_refs[(i - 1) % 2], (i - 1) * kt, q_tile)
    accumulate(d0 + n_diag - 1, p_refs[(n_diag - 1) % 2], (n_diag - 1) * kt, q_tile)

    o_t = acc_ref[...] * (1.0 / l_ref[...])
    y_ref[0] = (o_t.T * gate_ref[0].astype(F32)).astype(BF16)


def _fox_attention(qx, kx, vt, gate, *, bounded):
    b, h, s, _ = qx.shape
    dh = FOX_HEAD_DIM
    assert KV_TILE == SEQ_TILE and Q_TILE == 2 * KV_TILE and Q_TILE_BOUNDED % (2 * KV_TILE) == 0
    q_tile = Q_TILE_BOUNDED if bounded else Q_TILE
    probs = pltpu.VMEM((KV_TILE, q_tile), BF16)
    row = pltpu.VMEM((1, q_tile), F32)
    acc = pltpu.VMEM((dh, q_tile), F32)
    if bounded:
        body, scratch = _fox_attn_bounded_kernel, [probs, probs, row, acc]
    else:
        scores = pltpu.VMEM((KV_TILE, q_tile), F32)
        body, scratch = _fox_attn_kernel, [scores, scores, probs, probs, row, row, row, row, acc]
    return pl.pallas_call(
        body,
        grid=(b, h, s // q_tile),
        in_specs=[
            pl.BlockSpec((1, 1, q_tile, QK_WIDTH), lambda bi, hi, qi: (bi, hi, qi, 0)),
            pl.BlockSpec((1, 1, s, QK_WIDTH), lambda bi, hi, qi: (bi, hi, 0, 0)),
            pl.BlockSpec((1, 1, s // KV_TILE, dh, KV_TILE), lambda bi, hi, qi: (bi, hi, 0, 0, 0)),
            pl.BlockSpec((1, q_tile, dh), lambda bi, hi, qi: (bi, qi, hi)),
        ],
        out_specs=pl.BlockSpec((1, q_tile, dh), lambda bi, hi, qi: (bi, qi, hi)),
        out_shape=jax.ShapeDtypeStruct((b, s, h * dh), BF16),
        scratch_shapes=scratch,
        compiler_params=_params(3),
        name="fox_attention_bounded" if bounded else "fox_attention",
    )(qx, kx, vt, gate)


def _out_proj_kernel(x_ref, y_ref, w_ref, o_ref):
    o_ref[0] = x_ref[0] + _dot(y_ref[0], w_ref[...])


def _out_proj(x, y, w_out):
    b, s, d = x.shape
    t = SEQ_TILE
    e = y.shape[-1]
    return pl.pallas_call(
        _out_proj_kernel,
        grid=(b, s // t),
        in_specs=[
            pl.BlockSpec((1, t, d), lambda bi, si: (bi, si, 0)),
            pl.BlockSpec((1, t, e), lambda bi, si: (bi, si, 0)),
            _resident((e, d)),
        ],
        out_specs=pl.BlockSpec((1, t, d), lambda bi, si: (bi, si, 0)),
        out_shape=jax.ShapeDtypeStruct(x.shape, F32),
        compiler_params=_params(2),
        name="fox_out_proj",
    )(x, y, w_out.astype(BF16))


def _fox_layer(x, g, w_in, f_bias, q_norm, k_norm, w_out):
    qx, kx, vt, gate = _fox_proj(x, g, w_in, f_bias, q_norm, k_norm)
    qk_bound = (1.01 * FOX_HEAD_DIM ** 0.5 * LOG2_E) * jnp.max(jnp.abs(q_norm)) * jnp.max(jnp.abs(k_norm))
    y = lax.cond(qk_bound <= BOUNDED_LOGIT_LOG2,
                 functools.partial(_fox_attention, bounded=True),
                 functools.partial(_fox_attention, bounded=False),
                 qx, kx, vt, gate)
    return _out_proj(x, y, w_out)


def kernel(x, a_norm, a_w_in, a_conv_w, a_conv_b, a_ln_g, a_ln_b, a_w_out, b_norm, b_w_in, b_f_bias, b_q_norm, b_k_norm, b_w_out, c_norm, c_w_in, c_conv_w, c_w_out):
    depth = a_norm.shape[0] + b_norm.shape[0] + c_norm.shape[0]
    for i in range(depth):
        kind, j = i % 3, i // 3
        if kind == 0:
            x = _conformer_layer(x, a_norm[j], a_w_in[j], a_conv_w[j], a_conv_b[j],
                                 a_ln_g[j], a_ln_b[j], a_w_out[j])
        elif kind == 1:
            x = _fox_layer(x, b_norm[j], b_w_in[j], b_f_bias[j], b_q_norm[j], b_k_norm[j],
                           b_w_out[j])
        else:
            x = _short_conv_layer(x, c_norm[j], c_w_in[j], c_conv_w[j], c_w_out[j])
    return x
```

```python
import functools

import jax
import jax.numpy as jnp
from jax import lax
from jax.experimental import pallas as pl
from jax.experimental.pallas import tpu as pltpu

D_MODEL = 1024
D_INNER = 2048
CONF_KERNEL = 31
SHORT_KERNEL = 3
FOX_HEADS = 16
FOX_HEAD_DIM = 128
NORM_EPS = 1e-6

V7X_LANES = 128
V7X_SUBLANES = 8
V7X_VMEM_LIMIT_BYTES = 58 * 1024 * 1024

SEQ_TILE = 512
CHAN_CHUNK = 512
CONF_HALO = 32
SHORT_HALO = 8
SLABS_PER_Z = 2
KV_TILE = 512
Q_TILE = 2 * KV_TILE
Q_TILE_BOUNDED = 8 * KV_TILE
QK_WIDTH = 2 * FOX_HEAD_DIM
LOG2_E = 1.4426950408889634
BOUNDED_LOGIT_LOG2 = 50.0

BF16 = jnp.bfloat16
F32 = jnp.float32


def _sigmoid(x):
    return 1.0 / (1.0 + jnp.exp(-x))


def _silu(x):
    return x * _sigmoid(x)


def _rms_norm_rows(x, g):
    ms = jnp.mean(x * x, axis=-1, keepdims=True)
    return x * lax.rsqrt(ms + NORM_EPS) * g


def _dot(a, b):
    return jnp.dot(a, b, preferred_element_type=F32)


def _dot_nt(a, b):
    return lax.dot_general(a, b, (((1,), (1,)), ((), ())), preferred_element_type=F32)


def _split3(x):
    hi = x.astype(BF16)
    r = x - hi.astype(F32)
    mid = r.astype(BF16)
    lo = (r - mid.astype(F32)).astype(BF16)
    return hi, mid, lo


def _resident(shape):
    zeros = (0,) * len(shape)
    return pl.BlockSpec(shape, lambda *_: zeros, pipeline_mode=pl.Buffered(1))


def _params(n_axes):
    return pltpu.CompilerParams(
        dimension_semantics=("arbitrary",) * n_axes,
        vmem_limit_bytes=V7X_VMEM_LIMIT_BYTES,
    )


def _conformer_kernel(x_ref, g_ref, w_in_ref, wz_ref, cw_ref, cb_ref, lg_ref, lb_ref, w_out_ref,
                      o_ref, h_ref, u_ref, c_ref, zg_ref):
    si = pl.program_id(1)
    t = SEQ_TILE
    n_chunks = D_INNER // CHAN_CHUNK

    n_slabs = D_INNER // V7X_LANES
    slabs_per_chunk = CHAN_CHUNK // V7X_LANES

    @pl.when(si == 0)
    def _():
        u_ref[:, 0:CONF_HALO, :] = jnp.zeros((n_slabs, CONF_HALO, V7X_LANES), F32)

    x = x_ref[0]
    h_ref[...] = _rms_norm_rows(x, g_ref[...]).astype(BF16)

    for c in range(n_chunks):
        lo = c * CHAN_CHUNK
        val = _dot(h_ref[...], w_in_ref[:, lo:lo + CHAN_CHUNK])
        gate = _dot(h_ref[...], w_in_ref[:, D_INNER + lo:D_INNER + lo + CHAN_CHUNK])
        u = val * _sigmoid(gate)
        for j in range(slabs_per_chunk):
            u_ref[c * slabs_per_chunk + j, CONF_HALO:CONF_HALO + t, :] = (
                u[:, j * V7X_LANES:(j + 1) * V7X_LANES])

    first = CONF_HALO - (CONF_KERNEL - 1)
    rows = V7X_SUBLANES
    n_blocks = t // rows
    z_width = SLABS_PER_Z * V7X_LANES

    def conv_slab(sl):
        bias = jnp.broadcast_to(cb_ref[sl], (rows, V7X_LANES))
        taps = [jnp.broadcast_to(cw_ref[sl, k:k + 1, :], (rows, V7X_LANES))
                for k in range(CONF_KERNEL)]
        acc = {}
        for j in range(first, first + rows * (n_blocks - 1) + CONF_KERNEL):
            window = u_ref[sl, j:j + rows, :]
            lo_b = max(0, -(-(j - first - (CONF_KERNEL - 1)) // rows))
            hi_b = min(n_blocks - 1, (j - first) // rows)
            for b in range(lo_b, hi_b + 1):
                k = j - first - rows * b
                acc[b] = (bias if k == 0 else acc[b]) + taps[k] * window
                if k == CONF_KERNEL - 1:
                    c_ref[sl, rows * b:rows * (b + 1), :] = acc.pop(b)

    def conv_group(i, carry):
        for j in range(SLABS_PER_Z):
            conv_slab(i * SLABS_PER_Z + j)
        zg_ref[i] = _silu(_dot(h_ref[...], wz_ref[i]))
        return carry

    lax.fori_loop(0, n_slabs // SLABS_PER_Z, conv_group, 0)

    u_ref[:, 0:CONF_HALO, :] = u_ref[:, t:t + CONF_HALO, :]

    row_sum = c_ref[0]
    for sl in range(1, n_slabs):
        row_sum = row_sum + c_ref[sl]
    mu = jnp.sum(row_sum, axis=-1, keepdims=True) * (1.0 / D_INNER)
    sq_sum = jnp.square(c_ref[0] - mu)
    for sl in range(1, n_slabs):
        sq_sum = sq_sum + jnp.square(c_ref[sl] - mu)
    var = jnp.sum(sq_sum, axis=-1, keepdims=True) * (1.0 / D_INNER)
    rstd = lax.rsqrt(var + NORM_EPS)

    z_per_chunk = CHAN_CHUNK // z_width
    acc = x
    for c in range(n_chunks):
        lo = c * CHAN_CHUNK
        conv = jnp.concatenate(
            [c_ref[c * slabs_per_chunk + j] for j in range(slabs_per_chunk)], axis=-1)
        z_gate = jnp.concatenate(
            [zg_ref[c * z_per_chunk + j] for j in range(z_per_chunk)], axis=-1)
        ln = (conv - mu) * rstd * lg_ref[:, lo:lo + CHAN_CHUNK] + lb_ref[:, lo:lo + CHAN_CHUNK]
        y = (_silu(ln) * z_gate).astype(BF16)
        acc = acc + _dot(y, w_out_ref[lo:lo + CHAN_CHUNK, :])
    o_ref[0] = acc


def _conformer_layer(x, g, w_in, conv_w, conv_b, ln_g, ln_b, w_out):
    b, s, d = x.shape
    t = SEQ_TILE
    row = lambda n: _resident((1, n))
    n_slabs = D_INNER // V7X_LANES
    cw = conv_w.reshape(CONF_KERNEL, n_slabs, V7X_LANES).transpose(1, 0, 2)
    cb = conv_b.reshape(n_slabs, 1, V7X_LANES)
    z_width = SLABS_PER_Z * V7X_LANES
    n_z = D_INNER // z_width
    wz = w_in[:, 2 * D_INNER:].reshape(d, n_z, z_width).transpose(1, 0, 2).astype(BF16)
    return pl.pallas_call(
        _conformer_kernel,
        grid=(b, s // t),
        in_specs=[
            pl.BlockSpec((1, t, d), lambda bi, si: (bi, si, 0)),
            row(d),
            _resident((d, 2 * D_INNER)),
            _resident((n_z, d, z_width)),
            _resident((n_slabs, CONF_KERNEL, V7X_LANES)),
            _resident((n_slabs, 1, V7X_LANES)),
            row(D_INNER), row(D_INNER),
            _resident((D_INNER, d)),
        ],
        out_specs=pl.BlockSpec((1, t, d), lambda bi, si: (bi, si, 0)),
        out_shape=jax.ShapeDtypeStruct(x.shape, F32),
        scratch_shapes=[
            pltpu.VMEM((t, d), BF16),
            pltpu.VMEM((n_slabs, t + CONF_HALO, V7X_LANES), F32),
            pltpu.VMEM((n_slabs, t, V7X_LANES), F32),
            pltpu.VMEM((n_z, t, z_width), F32),
        ],
        compiler_params=_params(2),
        name="conformer_layer",
    )(x, g.reshape(1, d), w_in[:, :2 * D_INNER].astype(BF16), wz, cw, cb,
      ln_g.reshape(1, -1), ln_b.reshape(1, -1), w_out.astype(BF16))


def _short_conv_kernel(x_ref, g_ref, w_in_ref, cw_ref, w_out_ref, o_ref, h_ref, u_ref):
    si = pl.program_id(1)
    t = SEQ_TILE
    n_chunks = D_INNER // CHAN_CHUNK

    @pl.when(si == 0)
    def _():
        u_ref[0:SHORT_HALO, :] = jnp.zeros((SHORT_HALO, D_INNER), F32)

    x = x_ref[0]
    h_ref[...] = _rms_norm_rows(x, g_ref[...]).astype(BF16)

    acc = x
    for c in range(n_chunks):
        lo = c * CHAN_CHUNK
        cols = lambda part: slice(part * D_INNER + lo, part * D_INNER + lo + CHAN_CHUNK)
        u = _dot(h_ref[...], w_in_ref[:, cols(0)])
        c_gate = _dot(h_ref[...], w_in_ref[:, cols(2)])
        u_ref[SHORT_HALO:SHORT_HALO + t, lo:lo + CHAN_CHUNK] = c_gate * u
        conv = jnp.zeros((t, CHAN_CHUNK), F32)
        for k in range(SHORT_KERNEL):
            first = SHORT_HALO - (SHORT_KERNEL - 1) + k
            conv = conv + cw_ref[k:k + 1, lo:lo + CHAN_CHUNK] * u_ref[first:first + t, lo:lo + CHAN_CHUNK]
        b_gate = _dot(h_ref[...], w_in_ref[:, cols(1)])
        z = _dot(h_ref[...], w_in_ref[:, cols(3)])
        y = (b_gate * conv * _silu(z)).astype(BF16)
        acc = acc + _dot(y, w_out_ref[lo:lo + CHAN_CHUNK, :])
    o_ref[0] = acc

    u_ref[0:SHORT_HALO, :] = u_ref[t:t + SHORT_HALO, :]


def _short_conv_layer(x, g, w_in, conv_w, w_out):
    b, s, d = x.shape
    t = SEQ_TILE
    return pl.pallas_call(
        _short_conv_kernel,
        grid=(b, s // t),
        in_specs=[
            pl.BlockSpec((1, t, d), lambda bi, si: (bi, si, 0)),
            _resident((1, d)),
            _resident((d, 4 * D_INNER)),
            _resident((SHORT_KERNEL, D_INNER)),
            _resident((D_INNER, d)),
        ],
        out_specs=pl.BlockSpec((1, t, d), lambda bi, si: (bi, si, 0)),
        out_shape=jax.ShapeDtypeStruct(x.shape, F32),
        scratch_shapes=[
            pltpu.VMEM((t, d), BF16),
            pltpu.VMEM((t + SHORT_HALO, D_INNER), F32),
        ],
        compiler_params=_params(2),
        name="short_conv_layer",
    )(x, g.reshape(1, d), w_in.astype(BF16), conv_w, w_out.astype(BF16))


def _fox_proj_kernel(x_ref, g_ref, wqk_ref, wvt_ref, wz_ref, wf_hi_ref, wf_lo_ref, fb_ref,
                     qg_ref, kg_ref, qx_ref, kx_ref, vt_ref, gate_ref, h_ref, carry_ref):
    si = pl.program_id(1)
    t = SEQ_TILE
    dh = FOX_HEAD_DIM

    @pl.when(si == 0)
    def _():
        carry_ref[...] = jnp.zeros_like(carry_ref)

    x = x_ref[0]
    hf = _rms_norm_rows(x, g_ref[...])
    h_ref[...] = hf.astype(BF16)

    h_hi = h_ref[...]
    h_lo = (hf - h_hi.astype(F32)).astype(BF16)
    f_logit = (_dot(h_hi, wf_hi_ref[...]) + _dot(h_hi, wf_lo_ref[...])
               + _dot(h_lo, wf_hi_ref[...])) + fb_ref[...]
    log_f = jnp.minimum(f_logit, 0.0) - jnp.log1p(jnp.exp(-jnp.abs(f_logit)))
    rows = lax.broadcasted_iota(jnp.int32, (t, t), 0)
    cols = lax.broadcasted_iota(jnp.int32, (t, t), 1)
    tri = jnp.where(cols <= rows, 1.0, 0.0).astype(BF16)
    f_hi, f_mid, f_lo = _split3(log_f)
    c = (_dot(tri, f_hi) + _dot(tri, f_mid) + _dot(tri, f_lo)) + carry_ref[...]
    carry_ref[...] = c[t - 1:t, :]
    c_hi, c_mid, c_lo = _split3(c * LOG2_E)
    c_hi, c_mid, c_lo = c_hi.astype(F32), c_mid.astype(F32), c_lo.astype(F32)

    lane = lax.broadcasted_iota(jnp.int32, (t, dh), 1)
    scale = dh ** -0.5 * LOG2_E
    heads_per_chunk = CHAN_CHUNK // dh
    for c4 in range(D_INNER // CHAN_CHUNK):
        lo = c4 * CHAN_CHUNK
        q4 = _dot(h_ref[...], wqk_ref[:, lo:lo + CHAN_CHUNK])
        k4 = _dot(h_ref[...], wqk_ref[:, D_INNER + lo:D_INNER + lo + CHAN_CHUNK])
        for j in range(heads_per_chunk):
            hd = c4 * heads_per_chunk + j
            qh = _rms_norm_rows(q4[:, j * dh:(j + 1) * dh], qg_ref[...]) * scale
            kh = _rms_norm_rows(k4[:, j * dh:(j + 1) * dh], kg_ref[...])
            hi, mid, lw = c_hi[:, hd:hd + 1], c_mid[:, hd:hd + 1], c_lo[:, hd:hd + 1]
            q_ext = jnp.where(lane < 3, 1.0,
                              jnp.where(lane == 3, hi,
                                        jnp.where(lane == 4, mid,
                                                  jnp.where(lane == 5, lw, 0.0))))
            k_ext = jnp.where(lane == 0, -hi,
                              jnp.where(lane == 1, -mid,
                                        jnp.where(lane == 2, -lw,
                                                  jnp.where(lane < 6, 1.0, 0.0))))
            qx_ref[0, hd, :, 0:dh] = qh.astype(BF16)
            qx_ref[0, hd, :, dh:2 * dh] = q_ext.astype(BF16)
            kx_ref[0, hd, :, 0:dh] = kh.astype(BF16)
            kx_ref[0, hd, :, dh:2 * dh] = k_ext.astype(BF16)
        z = _dot(h_ref[...], wz_ref[:, lo:lo + CHAN_CHUNK])
        gate_ref[0, :, lo:lo + CHAN_CHUNK] = _silu(z).astype(BF16)
        vt = _dot_nt(wvt_ref[lo:lo + CHAN_CHUNK, :], h_ref[...])
        for j in range(heads_per_chunk):
            hd = c4 * heads_per_chunk + j
            vt_ref[0, hd, 0] = vt[j * dh:(j + 1) * dh, :].astype(BF16)


def _fox_proj(x, g, w_in, f_bias, q_norm, k_norm):
    b, s, d = x.shape
    t = SEQ_TILE
    h, dh = FOX_HEADS, FOX_HEAD_DIM
    e = D_INNER
    wqk = w_in[:, :2 * e].astype(BF16)
    wvt = w_in[:, 2 * e:3 * e].T.astype(BF16)
    wz = w_in[:, 3 * e:4 * e].astype(BF16)
    wf = w_in[:, 4 * e:]
    wf_hi = wf.astype(BF16)
    wf_lo = (wf - wf_hi.astype(F32)).astype(BF16)
    return pl.pallas_call(
        _fox_proj_kernel,
        grid=(b, s // t),
        in_specs=[
            pl.BlockSpec((1, t, d), lambda bi, si: (bi, si, 0)),
            _resident((1, d)),
            _resident((d, 2 * e)),
            _resident((e, d)),
            _resident((d, e)),
            _resident((d, h)), _resident((d, h)), _resident((1, h)),
            _resident((1, dh)), _resident((1, dh)),
        ],
        out_specs=[
            pl.BlockSpec((1, h, t, QK_WIDTH), lambda bi, si: (bi, 0, si, 0)),
            pl.BlockSpec((1, h, t, QK_WIDTH), lambda bi, si: (bi, 0, si, 0)),
            pl.BlockSpec((1, h, 1, dh, t), lambda bi, si: (bi, 0, si, 0, 0)),
            pl.BlockSpec((1, t, e), lambda bi, si: (bi, si, 0)),
        ],
        out_shape=[
            jax.ShapeDtypeStruct((b, h, s, QK_WIDTH), BF16),
            jax.ShapeDtypeStruct((b, h, s, QK_WIDTH), BF16),
            jax.ShapeDtypeStruct((b, h, s // t, dh, t), BF16),
            jax.ShapeDtypeStruct((b, s, e), BF16),
        ],
        scratch_shapes=[
            pltpu.VMEM((t, d), BF16),
            pltpu.VMEM((1, h), F32),
        ],
        compiler_params=_params(2),
        name="fox_proj",
    )(x, g.reshape(1, d), wqk, wvt, wz, wf_hi, wf_lo, f_bias.reshape(1, h),
      q_norm.reshape(1, dh), k_norm.reshape(1, dh))


def _fox_attn_kernel(qx_ref, kx_ref, vt_ref, gate_ref, y_ref,
                     s0_ref, s1_ref, p0_ref, p1_ref, a0_ref, a1_ref, m_ref, l_ref, acc_ref):
    qi = pl.program_id(2)
    kt = KV_TILE

    def scores(blk, s_ref, c0, c1):
        k0 = pl.multiple_of(blk * kt, kt)
        s_ref[:, c0:c1] = _dot_nt(kx_ref[0, 0, pl.ds(k0, kt), :], qx_ref[0, 0, c0:c1, :])

    def softmax(s_ref, p_ref, a_ref, c0, c1, masked):
        s = s_ref[:, c0:c1]
        if masked:
            kv_pos = lax.broadcasted_iota(jnp.int32, s.shape, 0)
            q_pos = lax.broadcasted_iota(jnp.int32, s.shape, 1)
            s = jnp.where(kv_pos <= q_pos, s, -jnp.inf)
        m_prev = m_ref[:, c0:c1]
        m_new = jnp.maximum(m_prev, jnp.max(s, axis=0, keepdims=True))
        p = jnp.exp2(s - m_new)
        alpha = jnp.exp2(m_prev - m_new)
        l_ref[:, c0:c1] = alpha * l_ref[:, c0:c1] + jnp.sum(p, axis=0, keepdims=True)
        m_ref[:, c0:c1] = m_new
        a_ref[:, c0:c1] = alpha
        p_ref[:, c0:c1] = p.astype(BF16)

    def accumulate(blk, p_ref, a_ref, c0, c1):
        acc_ref[:, c0:c1] = (a_ref[:, c0:c1] * acc_ref[:, c0:c1]
                             + _dot(vt_ref[0, 0, blk], p_ref[:, c0:c1]))

    m_ref[...] = jnp.full(m_ref.shape, -1e30, F32)
    l_ref[...] = jnp.zeros(l_ref.shape, F32)
    acc_ref[...] = jnp.zeros(acc_ref.shape, F32)
    p1_ref[...] = jnp.zeros(p1_ref.shape, BF16)
    a1_ref[...] = jnp.ones(a1_ref.shape, F32)

    full = (0, Q_TILE)
    scores(0, s0_ref, *full)

    def pair(t, carry):
        even = 2 * t
        scores(even + 1, s1_ref, *full)
        softmax(s0_ref, p0_ref, a0_ref, *full, masked=False)
        accumulate(jnp.maximum(even - 1, 0), p1_ref, a1_ref, *full)
        scores(even + 2, s0_ref, *full)
        softmax(s1_ref, p1_ref, a1_ref, *full, masked=False)
        accumulate(even, p0_ref, a0_ref, *full)
        return carry

    lax.fori_loop(0, qi, pair, 0)

    d0 = 2 * qi
    left, right = (0, kt), (kt, Q_TILE)
    scores(d0 + 1, s1_ref, *right)
    softmax(s0_ref, p0_ref, a0_ref, *left, masked=True)
    softmax(s0_ref, p0_ref, a0_ref, *right, masked=False)
    accumulate(jnp.maximum(d0 - 1, 0), p1_ref, a1_ref, *full)
    softmax(s1_ref, p1_ref, a1_ref, *right, masked=True)
    accumulate(d0, p0_ref, a0_ref, *full)
    accumulate(d0 + 1, p1_ref, a1_ref, *right)

    o_t = acc_ref[...] * (1.0 / l_ref[...])
    y_ref[0] = (o_t.T * gate_ref[0].astype(F32)).astype(BF16)


def _fox_attn_bounded_kernel(qx_ref, kx_ref, vt_ref, gate_ref, y_ref,
                             p0_ref, p1_ref, l_ref, acc_ref):
    qi = pl.program_id(2)
    kt = KV_TILE

    def probs(blk, p_ref, c0, c1, masked):
        k0 = pl.multiple_of(blk * kt, kt)
        s = _dot_nt(kx_ref[0, 0, pl.ds(k0, kt), :], qx_ref[0, 0, c0:c1, :])
        if masked:
            kv_pos = lax.broadcasted_iota(jnp.int32, s.shape, 0)
            q_pos = lax.broadcasted_iota(jnp.int32, s.shape, 1)
            s = jnp.where(kv_pos <= q_pos, s, -jnp.inf)
        p = jnp.exp2(s)
        l_ref[:, c0:c1] = l_ref[:, c0:c1] + jnp.sum(p, axis=0, keepdims=True)
        p_ref[:, c0:c1] = p.astype(BF16)

    def accumulate(blk, p_ref, c0, c1):
        acc_ref[:, c0:c1] = acc_ref[:, c0:c1] + _dot(vt_ref[0, 0, blk], p_ref[:, c0:c1])

    l_ref[...] = jnp.zeros(l_ref.shape, F32)
    acc_ref[...] = jnp.zeros(acc_ref.shape, F32)
    p1_ref[...] = jnp.zeros(p1_ref.shape, BF16)

    q_tile = p0_ref.shape[1]
    n_diag = q_tile // kt
    p_refs = (p0_ref, p1_ref)
    full = (0, q_tile)

    def pair(t, carry):
        even = 2 * t
        probs(even, p0_ref, *full, masked=False)
        accumulate(jnp.maximum(even - 1, 0), p1_ref, *full)
        probs(even + 1, p1_ref, *full, masked=False)
        accumulate(even, p0_ref, *full)
        return carry

    d0 = n_diag * qi
    lax.fori_loop(0, d0 // 2, pair, 0)

    def probs_diag(i):
        probs(d0 + i, p_refs[i % 2], i * kt, (i + 1) * kt, masked=True)
        if i + 1 < n_diag:
            probs(d0 + i, p_refs[i % 2], (i + 1) * kt, q_tile, masked=False)

    probs_diag(0)
    accumulate(jnp.maximum(d0 - 1, 0), p1_ref, *full)
    for i in range(1, n_diag):
        probs_diag(i)
        accumulate(d0 + i - 1, p_refs[(i - 1) % 2], (i - 1) * kt, q_tile)
    accumulate(d0 + n_diag - 1, p_refs[(n_diag - 1) % 2], (n_diag - 1) * kt, q_tile)

    o_t = acc_ref[...] * (1.0 / l_ref[...])
    y_ref[0] = (o_t.T * gate_ref[0].astype(F32)).astype(BF16)


def _fox_attention(qx, kx, vt, gate, *, bounded):
    b, h, s, _ = qx.shape
    dh = FOX_HEAD_DIM
    assert KV_TILE == SEQ_TILE and Q_TILE == 2 * KV_TILE and Q_TILE_BOUNDED % (2 * KV_TILE) == 0
    q_tile = Q_TILE_BOUNDED if bounded else Q_TILE
    probs = pltpu.VMEM((KV_TILE, q_tile), BF16)
    row = pltpu.VMEM((1, q_tile), F32)
    acc = pltpu.VMEM((dh, q_tile), F32)
    if bounded:
        body, scratch = _fox_attn_bounded_kernel, [probs, probs, row, acc]
    else:
        scores = pltpu.VMEM((KV_TILE, q_tile), F32)
        body, scratch = _fox_attn_kernel, [scores, scores, probs, probs, row, row, row, row, acc]
    return pl.pallas_call(
        body,
        grid=(b, h, s // q_tile),
        in_specs=[
            pl.BlockSpec((1, 1, q_tile, QK_WIDTH), lambda bi, hi, qi: (bi, hi, qi, 0)),
            pl.BlockSpec((1, 1, s, QK_WIDTH), lambda bi, hi, qi: (bi, hi, 0, 0)),
            pl.BlockSpec((1, 1, s // KV_TILE, dh, KV_TILE), lambda bi, hi, qi: (bi, hi, 0, 0, 0)),
            pl.BlockSpec((1, q_tile, dh), lambda bi, hi, qi: (bi, qi, hi)),
        ],
        out_specs=pl.BlockSpec((1, q_tile, dh), lambda bi, hi, qi: (bi, qi, hi)),
        out_shape=jax.ShapeDtypeStruct((b, s, h * dh), BF16),
        scratch_shapes=scratch,
        compiler_params=_params(3),
        name="fox_attention_bounded" if bounded else "fox_attention",
    )(qx, kx, vt, gate)


def _out_proj_kernel(x_ref, y_ref, w_ref, o_ref):
    o_ref[0] = x_ref[0] + _dot(y_ref[0], w_ref[...])


def _out_proj(x, y, w_out):
    b, s, d = x.shape
    t = SEQ_TILE
    e = y.shape[-1]
    return pl.pallas_call(
        _out_proj_kernel,
        grid=(b, s // t),
        in_specs=[
            pl.BlockSpec((1, t, d), lambda bi, si: (bi, si, 0)),
            pl.BlockSpec((1, t, e), lambda bi, si: (bi, si, 0)),
            _resident((e, d)),
        ],
        out_specs=pl.BlockSpec((1, t, d), lambda bi, si: (bi, si, 0)),
        out_shape=jax.ShapeDtypeStruct(x.shape, F32),
        compiler_params=_params(2),
        name="fox_out_proj",
    )(x, y, w_out.astype(BF16))


def _fox_layer(x, g, w_in, f_bias, q_norm, k_norm, w_out):
    qx, kx, vt, gate = _fox_proj(x, g, w_in, f_bias, q_norm, k_norm)
    qk_bound = (1.01 * FOX_HEAD_DIM ** 0.5 * LOG2_E) * jnp.max(jnp.abs(q_norm)) * jnp.max(jnp.abs(k_norm))
    y = lax.cond(qk_bound <= BOUNDED_LOGIT_LOG2,
                 functools.partial(_fox_attention, bounded=True),
                 functools.partial(_fox_attention, bounded=False),
                 qx, kx, vt, gate)
    return _out_proj(x, y, w_out)


def kernel(x, a_norm, a_w_in, a_conv_w, a_conv_b, a_ln_g, a_ln_b, a_w_out, b_norm, b_w_in, b_f_bias, b_q_norm, b_k_norm, b_w_out, c_norm, c_w_in, c_conv_w, c_w_out):
    depth = a_norm.shape[0] + b_norm.shape[0] + c_norm.shape[0]
    for i in range(depth):
        kind, j = i % 3, i // 3
        if kind == 0:
            x = _conformer_layer(x, a_norm[j], a_w_in[j], a_conv_w[j], a_conv_b[j],
                                 a_ln_g[j], a_ln_b[j], a_w_out[j])
        elif kind == 1:
            x = _fox_layer(x, b_norm[j], b_w_in[j], b_f_bias[j], b_q_norm[j], b_k_norm[j],
                           b_w_out[j])
        else:
            x = _short_conv_layer(x, c_norm[j], c_w_in[j], c_conv_w[j], c_w_out[j])
    return x
```

```python
import functools

import jax
import jax.numpy as jnp
from jax import lax
from jax.experimental import pallas as pl
from jax.experimental.pallas import tpu as pltpu

D_MODEL = 1024
D_INNER = 2048
CONF_KERNEL = 31
SHORT_KERNEL = 3
FOX_HEADS = 16
FOX_HEAD_DIM = 128
NORM_EPS = 1e-6

V7X_LANES = 128
V7X_SUBLANES = 8
V7X_VMEM_LIMIT_BYTES = 58 * 1024 * 1024

SEQ_TILE = 512
CHAN_CHUNK = 512
CONF_HALO = 32
SHORT_HALO = 8
CONV_ROWS = 64
SLABS_PER_Z = 2
KV_TILE = 512
Q_TILE = 2 * KV_TILE
Q_TILE_BOUNDED = 8 * KV_TILE
QK_WIDTH = 2 * FOX_HEAD_DIM
LOG2_E = 1.4426950408889634
BOUNDED_LOGIT_LOG2 = 50.0

BF16 = jnp.bfloat16
F32 = jnp.float32


def _sigmoid(x):
    return 1.0 / (1.0 + jnp.exp(-x))


def _silu(x):
    return x * _sigmoid(x)


def _rms_norm_rows(x, g):
    ms = jnp.mean(x * x, axis=-1, keepdims=True)
    return x * lax.rsqrt(ms + NORM_EPS) * g


def _dot(a, b):
    return jnp.dot(a, b, preferred_element_type=F32)


def _dot_nt(a, b):
    return lax.dot_general(a, b, (((1,), (1,)), ((), ())), preferred_element_type=F32)


def _split3(x):
    hi = x.astype(BF16)
    r = x - hi.astype(F32)
    mid = r.astype(BF16)
    lo = (r - mid.astype(F32)).astype(BF16)
    return hi, mid, lo


def _resident(shape):
    zeros = (0,) * len(shape)
    return pl.BlockSpec(shape, lambda *_: zeros, pipeline_mode=pl.Buffered(1))


def _params(n_axes):
    return pltpu.CompilerParams(
        dimension_semantics=("arbitrary",) * n_axes,
        vmem_limit_bytes=V7X_VMEM_LIMIT_BYTES,
    )


def _conformer_kernel(x_ref, g_ref, w_in_ref, wz_ref, cw_ref, cb_ref, lg_ref, lb_ref, w_out_ref,
                      o_ref, h_ref, u_ref, c_ref, zg_ref):
    si = pl.program_id(1)
    t = SEQ_TILE
    n_chunks = D_INNER // CHAN_CHUNK

    n_slabs = D_INNER // V7X_LANES
    slabs_per_chunk = CHAN_CHUNK // V7X_LANES

    @pl.when(si == 0)
    def _():
        u_ref[:, 0:CONF_HALO, :] = jnp.zeros((n_slabs, CONF_HALO, V7X_LANES), F32)

    x = x_ref[0]
    h_ref[...] = _rms_norm_rows(x, g_ref[...]).astype(BF16)

    for c in range(n_chunks):
        lo = c * CHAN_CHUNK
        val = _dot(h_ref[...], w_in_ref[:, lo:lo + CHAN_CHUNK])
        gate = _dot(h_ref[...], w_in_ref[:, D_INNER + lo:D_INNER + lo + CHAN_CHUNK])
        u = val * _sigmoid(gate)
        for j in range(slabs_per_chunk):
            u_ref[c * slabs_per_chunk + j, CONF_HALO:CONF_HALO + t, :] = (
                u[:, j * V7X_LANES:(j + 1) * V7X_LANES])

    first = CONF_HALO - (CONF_KERNEL - 1)

    def conv_slab(sl):
        for rb in range(t // CONV_ROWS):
            r0 = rb * CONV_ROWS
            acc = jnp.zeros((CONV_ROWS, V7X_LANES), F32)
            for k in range(CONF_KERNEL):
                acc = acc + (cw_ref[sl, k:k + 1, :]
                             * u_ref[sl, r0 + first + k:r0 + first + k + CONV_ROWS, :])
            c_ref[sl, r0:r0 + CONV_ROWS, :] = acc + cb_ref[sl]

    z_width = SLABS_PER_Z * V7X_LANES

    def conv_group(i, carry):
        for j in range(SLABS_PER_Z):
            conv_slab(i * SLABS_PER_Z + j)
        zg_ref[i] = _silu(_dot(h_ref[...], wz_ref[i]))
        return carry

    lax.fori_loop(0, n_slabs // SLABS_PER_Z, conv_group, 0)

    u_ref[:, 0:CONF_HALO, :] = u_ref[:, t:t + CONF_HALO, :]

    row_sum = c_ref[0]
    for sl in range(1, n_slabs):
        row_sum = row_sum + c_ref[sl]
    mu = jnp.sum(row_sum, axis=-1, keepdims=True) * (1.0 / D_INNER)
    sq_sum = jnp.square(c_ref[0] - mu)
    for sl in range(1, n_slabs):
        sq_sum = sq_sum + jnp.square(c_ref[sl] - mu)
    var = jnp.sum(sq_sum, axis=-1, keepdims=True) * (1.0 / D_INNER)
    rstd = lax.rsqrt(var + NORM_EPS)

    z_per_chunk = CHAN_CHUNK // z_width
    acc = x
    for c in range(n_chunks):
        lo = c * CHAN_CHUNK
        conv = jnp.concatenate(
            [c_ref[c * slabs_per_chunk + j] for j in range(slabs_per_chunk)], axis=-1)
        z_gate = jnp.concatenate(
            [zg_ref[c * z_per_chunk + j] for j in range(z_per_chunk)], axis=-1)
        ln = (conv - mu) * rstd * lg_ref[:, lo:lo + CHAN_CHUNK] + lb_ref[:, lo:lo + CHAN_CHUNK]
        y = (_silu(ln) * z_gate).astype(BF16)
        acc = acc + _dot(y, w_out_ref[lo:lo + CHAN_CHUNK, :])
    o_ref[0] = acc


def _conformer_layer(x, g, w_in, conv_w, conv_b, ln_g, ln_b, w_out):
    b, s, d = x.shape
    t = SEQ_TILE
    row = lambda n: _resident((1, n))
    n_slabs = D_INNER // V7X_LANES
    cw = conv_w.reshape(CONF_KERNEL, n_slabs, V7X_LANES).transpose(1, 0, 2)
    cb = conv_b.reshape(n_slabs, 1, V7X_LANES)
    z_width = SLABS_PER_Z * V7X_LANES
    n_z = D_INNER // z_width
    wz = w_in[:, 2 * D_INNER:].reshape(d, n_z, z_width).transpose(1, 0, 2).astype(BF16)
    return pl.pallas_call(
        _conformer_kernel,
        grid=(b, s // t),
        in_specs=[
            pl.BlockSpec((1, t, d), lambda bi, si: (bi, si, 0)),
            row(d),
            _resident((d, 2 * D_INNER)),
            _resident((n_z, d, z_width)),
            _resident((n_slabs, CONF_KERNEL, V7X_LANES)),
            _resident((n_slabs, 1, V7X_LANES)),
            row(D_INNER), row(D_INNER),
            _resident((D_INNER, d)),
        ],
        out_specs=pl.BlockSpec((1, t, d), lambda bi, si: (bi, si, 0)),
        out_shape=jax.ShapeDtypeStruct(x.shape, F32),
        scratch_shapes=[
            pltpu.VMEM((t, d), BF16),
            pltpu.VMEM((n_slabs, t + CONF_HALO, V7X_LANES), F32),
            pltpu.VMEM((n_slabs, t, V7X_LANES), F32),
            pltpu.VMEM((n_z, t, z_width), F32),
        ],
        compiler_params=_params(2),
        name="conformer_layer",
    )(x, g.reshape(1, d), w_in[:, :2 * D_INNER].astype(BF16), wz, cw, cb,
      ln_g.reshape(1, -1), ln_b.reshape(1, -1), w_out.astype(BF16))


def _short_conv_kernel(x_ref, g_ref, w_in_ref, cw_ref, w_out_ref, o_ref, h_ref, u_ref):
    si = pl.program_id(1)
    t = SEQ_TILE
    n_chunks = D_INNER // CHAN_CHUNK

    @pl.when(si == 0)
    def _():
        u_ref[0:SHORT_HALO, :] = jnp.zeros((SHORT_HALO, D_INNER), F32)

    x = x_ref[0]
    h_ref[...] = _rms_norm_rows(x, g_ref[...]).astype(BF16)

    acc = x
    for c in range(n_chunks):
        lo = c * CHAN_CHUNK
        cols = lambda part: slice(part * D_INNER + lo, part * D_INNER + lo + CHAN_CHUNK)
        u = _dot(h_ref[...], w_in_ref[:, cols(0)])
        c_gate = _dot(h_ref[...], w_in_ref[:, cols(2)])
        u_ref[SHORT_HALO:SHORT_HALO + t, lo:lo + CHAN_CHUNK] = c_gate * u
        conv = jnp.zeros((t, CHAN_CHUNK), F32)
        for k in range(SHORT_KERNEL):
            first = SHORT_HALO - (SHORT_KERNEL - 1) + k
            conv = conv + cw_ref[k:k + 1, lo:lo + CHAN_CHUNK] * u_ref[first:first + t, lo:lo + CHAN_CHUNK]
        b_gate = _dot(h_ref[...], w_in_ref[:, cols(1)])
        z = _dot(h_ref[...], w_in_ref[:, cols(3)])
        y = (b_gate * conv * _silu(z)).astype(BF16)
        acc = acc + _dot(y, w_out_ref[lo:lo + CHAN_CHUNK, :])
    o_ref[0] = acc

    u_ref[0:SHORT_HALO, :] = u_ref[t:t + SHORT_HALO, :]


def _short_conv_layer(x, g, w_in, conv_w, w_out):
    b, s, d = x.shape
    t = SEQ_TILE
    return pl.pallas_call(
        _short_conv_kernel,
        grid=(b, s // t),
        in_specs=[
            pl.BlockSpec((1, t, d), lambda bi, si: (bi, si, 0)),
            _resident((1, d)),
            _resident((d, 4 * D_INNER)),
            _resident((SHORT_KERNEL, D_INNER)),
            _resident((D_INNER, d)),
        ],
        out_specs=pl.BlockSpec((1, t, d), lambda bi, si: (bi, si, 0)),
        out_shape=jax.ShapeDtypeStruct(x.shape, F32),
        scratch_shapes=[
            pltpu.VMEM((t, d), BF16),
            pltpu.VMEM((t + SHORT_HALO, D_INNER), F32),
        ],
        compiler_params=_params(2),
        name="short_conv_layer",
    )(x, g.reshape(1, d), w_in.astype(BF16), conv_w, w_out.astype(BF16))


def _fox_proj_kernel(x_ref, g_ref, wqk_ref, wvt_ref, wz_ref, wf_hi_ref, wf_lo_ref, fb_ref,
                     qg_ref, kg_ref, qx_ref, kx_ref, vt_ref, gate_ref, h_ref, carry_ref):
    si = pl.program_id(1)
    t = SEQ_TILE
    dh = FOX_HEAD_DIM

    @pl.when(si == 0)
    def _():
        carry_ref[...] = jnp.zeros_like(carry_ref)

    x = x_ref[0]
    hf = _rms_norm_rows(x, g_ref[...])
    h_ref[...] = hf.astype(BF16)

    h_hi = h_ref[...]
    h_lo = (hf - h_hi.astype(F32)).astype(BF16)
    f_logit = (_dot(h_hi, wf_hi_ref[...]) + _dot(h_hi, wf_lo_ref[...])
               + _dot(h_lo, wf_hi_ref[...])) + fb_ref[...]
    log_f = jnp.minimum(f_logit, 0.0) - jnp.log1p(jnp.exp(-jnp.abs(f_logit)))
    rows = lax.broadcasted_iota(jnp.int32, (t, t), 0)
    cols = lax.broadcasted_iota(jnp.int32, (t, t), 1)
    tri = jnp.where(cols <= rows, 1.0, 0.0).astype(BF16)
    f_hi, f_mid, f_lo = _split3(log_f)
    c = (_dot(tri, f_hi) + _dot(tri, f_mid) + _dot(tri, f_lo)) + carry_ref[...]
    carry_ref[...] = c[t - 1:t, :]
    c_hi, c_mid, c_lo = _split3(c * LOG2_E)
    c_hi, c_mid, c_lo = c_hi.astype(F32), c_mid.astype(F32), c_lo.astype(F32)

    lane = lax.broadcasted_iota(jnp.int32, (t, dh), 1)
    scale = dh ** -0.5 * LOG2_E
    heads_per_chunk = CHAN_CHUNK // dh
    for c4 in range(D_INNER // CHAN_CHUNK):
        lo = c4 * CHAN_CHUNK
        q4 = _dot(h_ref[...], wqk_ref[:, lo:lo + CHAN_CHUNK])
        k4 = _dot(h_ref[...], wqk_ref[:, D_INNER + lo:D_INNER + lo + CHAN_CHUNK])
        for j in range(heads_per_chunk):
            hd = c4 * heads_per_chunk + j
            qh = _rms_norm_rows(q4[:, j * dh:(j + 1) * dh], qg_ref[...]) * scale
            kh = _rms_norm_rows(k4[:, j * dh:(j + 1) * dh], kg_ref[...])
            hi, mid, lw = c_hi[:, hd:hd + 1], c_mid[:, hd:hd + 1], c_lo[:, hd:hd + 1]
            q_ext = jnp.where(lane < 3, 1.0,
                              jnp.where(lane == 3, hi,
                                        jnp.where(lane == 4, mid,
                                                  jnp.where(lane == 5, lw, 0.0))))
            k_ext = jnp.where(lane == 0, -hi,
                              jnp.where(lane == 1, -mid,
                                        jnp.where(lane == 2, -lw,
                                                  jnp.where(lane < 6, 1.0, 0.0))))
            qx_ref[0, hd, :, 0:dh] = qh.astype(BF16)
            qx_ref[0, hd, :, dh:2 * dh] = q_ext.astype(BF16)
            kx_ref[0, hd, :, 0:dh] = kh.astype(BF16)
            kx_ref[0, hd, :, dh:2 * dh] = k_ext.astype(BF16)
        z = _dot(h_ref[...], wz_ref[:, lo:lo + CHAN_CHUNK])
        gate_ref[0, :, lo:lo + CHAN_CHUNK] = _silu(z).astype(BF16)
        vt = _dot_nt(wvt_ref[lo:lo + CHAN_CHUNK, :], h_ref[...])
        for j in range(heads_per_chunk):
            hd = c4 * heads_per_chunk + j
            vt_ref[0, hd, 0] = vt[j * dh:(j + 1) * dh, :].astype(BF16)


def _fox_proj(x, g, w_in, f_bias, q_norm, k_norm):
    b, s, d = x.shape
    t = SEQ_TILE
    h, dh = FOX_HEADS, FOX_HEAD_DIM
    e = D_INNER
    wqk = w_in[:, :2 * e].astype(BF16)
    wvt = w_in[:, 2 * e:3 * e].T.astype(BF16)
    wz = w_in[:, 3 * e:4 * e].astype(BF16)
    wf = w_in[:, 4 * e:]
    wf_hi = wf.astype(BF16)
    wf_lo = (wf - wf_hi.astype(F32)).astype(BF16)
    return pl.pallas_call(
        _fox_proj_kernel,
        grid=(b, s // t),
        in_specs=[
            pl.BlockSpec((1, t, d), lambda bi, si: (bi, si, 0)),
            _resident((1, d)),
            _resident((d, 2 * e)),
            _resident((e, d)),
            _resident((d, e)),
            _resident((d, h)), _resident((d, h)), _resident((1, h)),
            _resident((1, dh)), _resident((1, dh)),
        ],
        out_specs=[
            pl.BlockSpec((1, h, t, QK_WIDTH), lambda bi, si: (bi, 0, si, 0)),
            pl.BlockSpec((1, h, t, QK_WIDTH), lambda bi, si: (bi, 0, si, 0)),
            pl.BlockSpec((1, h, 1, dh, t), lambda bi, si: (bi, 0, si, 0, 0)),
            pl.BlockSpec((1, t, e), lambda bi, si: (bi, si, 0)),
        ],
        out_shape=[
            jax.ShapeDtypeStruct((b, h, s, QK_WIDTH), BF16),
            jax.ShapeDtypeStruct((b, h, s, QK_WIDTH), BF16),
            jax.ShapeDtypeStruct((b, h, s // t, dh, t), BF16),
            jax.ShapeDtypeStruct((b, s, e), BF16),
        ],
        scratch_shapes=[
            pltpu.VMEM((t, d), BF16),
            pltpu.VMEM((1, h), F32),
        ],
        compiler_params=_params(2),
        name="fox_proj",
    )(x, g.reshape(1, d), wqk, wvt, wz, wf_hi, wf_lo, f_bias.reshape(1, h),
      q_norm.reshape(1, dh), k_norm.reshape(1, dh))


def _fox_attn_kernel(qx_ref, kx_ref, vt_ref, gate_ref, y_ref,
                     s0_ref, s1_ref, p0_ref, p1_ref, a0_ref, a1_ref, m_ref, l_ref, acc_ref):
    qi = pl.program_id(2)
    kt = KV_TILE

    def scores(blk, s_ref, c0, c1):
        k0 = pl.multiple_of(blk * kt, kt)
        s_ref[:, c0:c1] = _dot_nt(kx_ref[0, 0, pl.ds(k0, kt), :], qx_ref[0, 0, c0:c1, :])

    def softmax(s_ref, p_ref, a_ref, c0, c1, masked):
        s = s_ref[:, c0:c1]
        if masked:
            kv_pos = lax.broadcasted_iota(jnp.int32, s.shape, 0)
            q_pos = lax.broadcasted_iota(jnp.int32, s.shape, 1)
            s = jnp.where(kv_pos <= q_pos, s, -jnp.inf)
        m_prev = m_ref[:, c0:c1]
        m_new = jnp.maximum(m_prev, jnp.max(s, axis=0, keepdims=True))
        p = jnp.exp2(s - m_new)
        alpha = jnp.exp2(m_prev - m_new)
        l_ref[:, c0:c1] = alpha * l_ref[:, c0:c1] + jnp.sum(p, axis=0, keepdims=True)
        m_ref[:, c0:c1] = m_new
        a_ref[:, c0:c1] = alpha
        p_ref[:, c0:c1] = p.astype(BF16)

    def accumulate(blk, p_ref, a_ref, c0, c1):
        acc_ref[:, c0:c1] = (a_ref[:, c0:c1] * acc_ref[:, c0:c1]
                             + _dot(vt_ref[0, 0, blk], p_ref[:, c0:c1]))

    m_ref[...] = jnp.full(m_ref.shape, -1e30, F32)
    l_ref[...] = jnp.zeros(l_ref.shape, F32)
    acc_ref[...] = jnp.zeros(acc_ref.shape, F32)
    p1_ref[...] = jnp.zeros(p1_ref.shape, BF16)
    a1_ref[...] = jnp.ones(a1_ref.shape, F32)

    full = (0, Q_TILE)
    scores(0, s0_ref, *full)

    def pair(t, carry):
        even = 2 * t
        scores(even + 1, s1_ref, *full)
        softmax(s0_ref, p0_ref, a0_ref, *full, masked=False)
        accumulate(jnp.maximum(even - 1, 0), p1_ref, a1_ref, *full)
        scores(even + 2, s0_ref, *full)
        softmax(s1_ref, p1_ref, a1_ref, *full, masked=False)
        accumulate(even, p0_ref, a0_ref, *full)
        return carry

    lax.fori_loop(0, qi, pair, 0)

    d0 = 2 * qi
    left, right = (0, kt), (kt, Q_TILE)
    scores(d0 + 1, s1_ref, *right)
    softmax(s0_ref, p0_ref, a0_ref, *left, masked=True)
    softmax(s0_ref, p0_ref, a0_ref, *right, masked=False)
    accumulate(jnp.maximum(d0 - 1, 0), p1_ref, a1_ref, *full)
    softmax(s1_ref, p1_ref, a1_ref, *right, masked=True)
    accumulate(d0, p0_ref, a0_ref, *full)
    accumulate(d0 + 1, p1_ref, a1_ref, *right)

    o_t = acc_ref[...] * (1.0 / l_ref[...])
    y_ref[0] = (o_t.T * gate_ref[0].astype(F32)).astype(BF16)


def _fox_attn_bounded_kernel(qx_ref, kx_ref, vt_ref, gate_ref, y_ref,
                             p0_ref, p1_ref, l_ref, acc_ref):
    qi = pl.program_id(2)
    kt = KV_TILE

    def probs(blk, p_ref, c0, c1, masked):
        k0 = pl.multiple_of(blk * kt, kt)
        s = _dot_nt(kx_ref[0, 0, pl.ds(k0, kt), :], qx_ref[0, 0, c0:c1, :])
        if masked:
            kv_pos = lax.broadcasted_iota(jnp.int32, s.shape, 0)
            q_pos = lax.broadcasted_iota(jnp.int32, s.shape, 1)
            s = jnp.where(kv_pos <= q_pos, s, -jnp.inf)
        p = jnp.exp2(s)
        l_ref[:, c0:c1] = l_ref[:, c0:c1] + jnp.sum(p, axis=0, keepdims=True)
        p_ref[:, c0:c1] = p.astype(BF16)

    def accumulate(blk, p_ref, c0, c1):
        acc_ref[:, c0:c1] = acc_ref[:, c0:c1] + _dot(vt_ref[0, 0, blk], p_ref[:, c0:c1])

    l_ref[...] = jnp.zeros(l_ref.shape, F32)
    acc_ref[...] = jnp.zeros(acc_ref.shape, F32)
    p1_ref[...] = jnp.zeros(p1_ref.shape, BF16)

    q_tile = p0_ref.shape[1]
    n_diag = q_tile // kt
    p_refs = (p0_ref, p1_ref)
    full = (0, q_tile)

    def pair(t, carry):
        even = 2 * t
        probs(even, p0_ref, *full, masked=False)
        accumulate(jnp.maximum(even - 1, 0), p1_ref, *full)
        probs(even + 1, p1_ref, *full, masked=False)
        accumulate(even, p0_ref, *full)
        return carry

    d0 = n_diag * qi
    lax.fori_loop(0, d0 // 2, pair, 0)

    def probs_diag(i):
        probs(d0 + i, p_refs[i % 2], i * kt, (i + 1) * kt, masked=True)
        if i + 1 < n_diag:
            probs(d0 + i, p_refs[i % 2], (i + 1) * kt, q_tile, masked=False)

    probs_diag(0)
    accumulate(jnp.maximum(d0 - 1, 0), p1_ref, *full)
    for i in range(1, n_diag):
        probs_diag(i)
        accumulate(d0 + i - 1, p_refs[(i - 1) % 2], (i - 1) * kt, q_tile)
    accumulate(d0 + n_diag - 1, p_refs[(n_diag - 1) % 2], (n_diag - 1) * kt, q_tile)

    o_t = acc_ref[...] * (1.0 / l_ref[...])
    y_ref[0] = (o_t.T * gate_ref[0].astype(F32)).astype(BF16)


def _fox_attention(qx, kx, vt, gate, *, bounded):
    b, h, s, _ = qx.shape
    dh = FOX_HEAD_DIM
    assert KV_TILE == SEQ_TILE and Q_TILE == 2 * KV_TILE and Q_TILE_BOUNDED % (2 * KV_TILE) == 0
    q_tile = Q_TILE_BOUNDED if bounded else Q_TILE
    probs = pltpu.VMEM((KV_TILE, q_tile), BF16)
    row = pltpu.VMEM((1, q_tile), F32)
    acc = pltpu.VMEM((dh, q_tile), F32)
    if bounded:
        body, scratch = _fox_attn_bounded_kernel, [probs, probs, row, acc]
    else:
        scores = pltpu.VMEM((KV_TILE, q_tile), F32)
        body, scratch = _fox_attn_kernel, [scores, scores, probs, probs, row, row, row, row, acc]
    return pl.pallas_call(
        body,
        grid=(b, h, s // q_tile),
        in_specs=[
            pl.BlockSpec((1, 1, q_tile, QK_WIDTH), lambda bi, hi, qi: (bi, hi, qi, 0)),
            pl.BlockSpec((1, 1, s, QK_WIDTH), lambda bi, hi, qi: (bi, hi, 0, 0)),
            pl.BlockSpec((1, 1, s // KV_TILE, dh, KV_TILE), lambda bi, hi, qi: (bi, hi, 0, 0, 0)),
            pl.BlockSpec((1, q_tile, dh), lambda bi, hi, qi: (bi, qi, hi)),
        ],
        out_specs=pl.BlockSpec((1, q_tile, dh), lambda bi, hi, qi: (bi, qi, hi)),
        out_shape=jax.ShapeDtypeStruct((b, s, h * dh), BF16),
        scratch_shapes=scratch,
        compiler_params=_params(3),
        name="fox_attention_bounded" if bounded else "fox_attention",
    )(qx, kx, vt, gate)


def _out_proj_kernel(x_ref, y_ref, w_ref, o_ref):
    o_ref[0] = x_ref[0] + _dot(y_ref[0], w_ref[...])


def _out_proj(x, y, w_out):
    b, s, d = x.shape
    t = SEQ_TILE
    e = y.shape[-1]
    return pl.pallas_call(
        _out_proj_kernel,
        grid=(b, s // t),
        in_specs=[
            pl.BlockSpec((1, t, d), lambda bi, si: (bi, si, 0)),
            pl.BlockSpec((1, t, e), lambda bi, si: (bi, si, 0)),
            _resident((e, d)),
        ],
        out_specs=pl.BlockSpec((1, t, d), lambda bi, si: (bi, si, 0)),
        out_shape=jax.ShapeDtypeStruct(x.shape, F32),
        compiler_params=_params(2),
        name="fox_out_proj",
    )(x, y, w_out.astype(BF16))


def _fox_layer(x, g, w_in, f_bias, q_norm, k_norm, w_out):
    qx, kx, vt, gate = _fox_proj(x, g, w_in, f_bias, q_norm, k_norm)
    qk_bound = (1.01 * FOX_HEAD_DIM ** 0.5 * LOG2_E) * jnp.max(jnp.abs(q_norm)) * jnp.max(jnp.abs(k_norm))
    y = lax.cond(qk_bound <= BOUNDED_LOGIT_LOG2,
                 functools.partial(_fox_attention, bounded=True),
                 functools.partial(_fox_attention, bounded=False),
                 qx, kx, vt, gate)
    return _out_proj(x, y, w_out)


def kernel(x, a_norm, a_w_in, a_conv_w, a_conv_b, a_ln_g, a_ln_b, a_w_out, b_norm, b_w_in, b_f_bias, b_q_norm, b_k_norm, b_w_out, c_norm, c_w_in, c_conv_w, c_w_out):
    depth = a_norm.shape[0] + b_norm.shape[0] + c_norm.shape[0]
    for i in range(depth):
        kind, j = i % 3, i // 3
        if kind == 0:
            x = _conformer_layer(x, a_norm[j], a_w_in[j], a_conv_w[j], a_conv_b[j],
                                 a_ln_g[j], a_ln_b[j], a_w_out[j])
        elif kind == 1:
            x = _fox_layer(x, b_norm[j], b_w_in[j], b_f_bias[j], b_q_norm[j], b_k_norm[j],
                           b_w_out[j])
        else:
            x = _short_conv_layer(x, c_norm[j], c_w_in[j], c_conv_w[j], c_w_out[j])
    return x
```

```python
import functools

import jax
import jax.numpy as jnp
from jax import lax
from jax.experimental import pallas as pl
from jax.experimental.pallas import tpu as pltpu

D_MODEL = 1024
D_INNER = 2048
CONF_KERNEL = 31
SHORT_KERNEL = 3
FOX_HEADS = 16
FOX_HEAD_DIM = 128
NORM_EPS = 1e-6

V7X_LANES = 128
V7X_SUBLANES = 8
V7X_VMEM_LIMIT_BYTES = 58 * 1024 * 1024

SEQ_TILE = 512
CHAN_CHUNK = 512
CONF_HALO = 32
SHORT_HALO = 8
CONV_ROWS = 64
KV_TILE = 512
Q_TILE = 2 * KV_TILE
Q_TILE_BOUNDED = 8 * KV_TILE
QK_WIDTH = 2 * FOX_HEAD_DIM
LOG2_E = 1.4426950408889634
BOUNDED_LOGIT_LOG2 = 50.0

BF16 = jnp.bfloat16
F32 = jnp.float32


def _sigmoid(x):
    return 1.0 / (1.0 + jnp.exp(-x))


def _silu(x):
    return x * _sigmoid(x)


def _rms_norm_rows(x, g):
    ms = jnp.mean(x * x, axis=-1, keepdims=True)
    return x * lax.rsqrt(ms + NORM_EPS) * g


def _dot(a, b):
    return jnp.dot(a, b, preferred_element_type=F32)


def _dot_nt(a, b):
    return lax.dot_general(a, b, (((1,), (1,)), ((), ())), preferred_element_type=F32)


def _split3(x):
    hi = x.astype(BF16)
    r = x - hi.astype(F32)
    mid = r.astype(BF16)
    lo = (r - mid.astype(F32)).astype(BF16)
    return hi, mid, lo


def _resident(shape):
    zeros = (0,) * len(shape)
    return pl.BlockSpec(shape, lambda *_: zeros, pipeline_mode=pl.Buffered(1))


def _resident_layer(shape, j):
    zeros = (0,) * len(shape)
    return pl.BlockSpec((None,) + tuple(shape), lambda *_: (j,) + zeros,
                        pipeline_mode=pl.Buffered(1))


def _params(n_axes):
    return pltpu.CompilerParams(
        dimension_semantics=("arbitrary",) * n_axes,
        vmem_limit_bytes=V7X_VMEM_LIMIT_BYTES,
    )


def _conformer_kernel(x_ref, g_ref, w_in_ref, cw_ref, cb_ref, lg_ref, lb_ref, w_out_ref,
                      o_ref, h_ref, u_ref, c_ref):
    si = pl.program_id(1)
    t = SEQ_TILE
    n_chunks = D_INNER // CHAN_CHUNK

    n_slabs = D_INNER // V7X_LANES
    slabs_per_chunk = CHAN_CHUNK // V7X_LANES

    @pl.when(si == 0)
    def _():
        u_ref[:, 0:CONF_HALO, :] = jnp.zeros((n_slabs, CONF_HALO, V7X_LANES), F32)

    x = x_ref[0]
    h_ref[...] = _rms_norm_rows(x, g_ref[...]).astype(BF16)

    for c in range(n_chunks):
        lo = c * CHAN_CHUNK
        val = _dot(h_ref[...], w_in_ref[:, lo:lo + CHAN_CHUNK])
        gate = _dot(h_ref[...], w_in_ref[:, D_INNER + lo:D_INNER + lo + CHAN_CHUNK])
        u = val * _sigmoid(gate)
        for j in range(slabs_per_chunk):
            u_ref[c * slabs_per_chunk + j, CONF_HALO:CONF_HALO + t, :] = (
                u[:, j * V7X_LANES:(j + 1) * V7X_LANES])

    first = CONF_HALO - (CONF_KERNEL - 1)

    def conv_slab(sl, carry):
        for rb in range(t // CONV_ROWS):
            r0 = rb * CONV_ROWS
            acc = jnp.zeros((CONV_ROWS, V7X_LANES), F32)
            for k in range(CONF_KERNEL):
                acc = acc + (cw_ref[sl, k:k + 1, :]
                             * u_ref[sl, r0 + first + k:r0 + first + k + CONV_ROWS, :])
            c_ref[sl, r0:r0 + CONV_ROWS, :] = acc + cb_ref[sl]
        return carry

    lax.fori_loop(0, n_slabs, conv_slab, 0)

    u_ref[:, 0:CONF_HALO, :] = u_ref[:, t:t + CONF_HALO, :]

    row_sum = c_ref[0]
    for sl in range(1, n_slabs):
        row_sum = row_sum + c_ref[sl]
    mu = jnp.sum(row_sum, axis=-1, keepdims=True) * (1.0 / D_INNER)
    sq_sum = jnp.square(c_ref[0] - mu)
    for sl in range(1, n_slabs):
        sq_sum = sq_sum + jnp.square(c_ref[sl] - mu)
    var = jnp.sum(sq_sum, axis=-1, keepdims=True) * (1.0 / D_INNER)
    rstd = lax.rsqrt(var + NORM_EPS)

    acc = x
    for c in range(n_chunks):
        lo = c * CHAN_CHUNK
        z = _dot(h_ref[...], w_in_ref[:, 2 * D_INNER + lo:2 * D_INNER + lo + CHAN_CHUNK])
        conv = jnp.concatenate(
            [c_ref[c * slabs_per_chunk + j] for j in range(slabs_per_chunk)], axis=-1)
        ln = (conv - mu) * rstd * lg_ref[:, lo:lo + CHAN_CHUNK] + lb_ref[:, lo:lo + CHAN_CHUNK]
        y = (_silu(ln) * _silu(z)).astype(BF16)
        acc = acc + _dot(y, w_out_ref[lo:lo + CHAN_CHUNK, :])
    o_ref[0] = acc


def _conformer_layer(x, j, g, w_in_stack, conv_w, conv_b, ln_g, ln_b, w_out_stack):
    b, s, d = x.shape
    t = SEQ_TILE
    row = lambda n: _resident((1, n))
    n_slabs = D_INNER // V7X_LANES
    cw = conv_w.reshape(CONF_KERNEL, n_slabs, V7X_LANES).transpose(1, 0, 2)
    cb = conv_b.reshape(n_slabs, 1, V7X_LANES)
    return pl.pallas_call(
        _conformer_kernel,
        grid=(b, s // t),
        in_specs=[
            pl.BlockSpec((1, t, d), lambda bi, si: (bi, si, 0)),
            row(d),
            _resident_layer((d, 3 * D_INNER), j),
            _resident((n_slabs, CONF_KERNEL, V7X_LANES)),
            _resident((n_slabs, 1, V7X_LANES)),
            row(D_INNER), row(D_INNER),
            _resident_layer((D_INNER, d), j),
        ],
        out_specs=pl.BlockSpec((1, t, d), lambda bi, si: (bi, si, 0)),
        out_shape=jax.ShapeDtypeStruct(x.shape, F32),
        scratch_shapes=[
            pltpu.VMEM((t, d), BF16),
            pltpu.VMEM((n_slabs, t + CONF_HALO, V7X_LANES), F32),
            pltpu.VMEM((n_slabs, t, V7X_LANES), F32),
        ],
        compiler_params=_params(2),
        name="conformer_layer",
    )(x, g.reshape(1, d), w_in_stack, cw, cb,
      ln_g.reshape(1, -1), ln_b.reshape(1, -1), w_out_stack)


def _short_conv_kernel(x_ref, g_ref, w_in_ref, cw_ref, w_out_ref, o_ref, h_ref, u_ref):
    si = pl.program_id(1)
    t = SEQ_TILE
    n_chunks = D_INNER // CHAN_CHUNK

    @pl.when(si == 0)
    def _():
        u_ref[0:SHORT_HALO, :] = jnp.zeros((SHORT_HALO, D_INNER), F32)

    x = x_ref[0]
    h_ref[...] = _rms_norm_rows(x, g_ref[...]).astype(BF16)

    acc = x
    for c in range(n_chunks):
        lo = c * CHAN_CHUNK
        cols = lambda part: slice(part * D_INNER + lo, part * D_INNER + lo + CHAN_CHUNK)
        u = _dot(h_ref[...], w_in_ref[:, cols(0)])
        c_gate = _dot(h_ref[...], w_in_ref[:, cols(2)])
        u_ref[SHORT_HALO:SHORT_HALO + t, lo:lo + CHAN_CHUNK] = c_gate * u
        conv = jnp.zeros((t, CHAN_CHUNK), F32)
        for k in range(SHORT_KERNEL):
            first = SHORT_HALO - (SHORT_KERNEL - 1) + k
            conv = conv + cw_ref[k:k + 1, lo:lo + CHAN_CHUNK] * u_ref[first:first + t, lo:lo + CHAN_CHUNK]
        b_gate = _dot(h_ref[...], w_in_ref[:, cols(1)])
        z = _dot(h_ref[...], w_in_ref[:, cols(3)])
        y = (b_gate * conv * _silu(z)).astype(BF16)
        acc = acc + _dot(y, w_out_ref[lo:lo + CHAN_CHUNK, :])
    o_ref[0] = acc

    u_ref[0:SHORT_HALO, :] = u_ref[t:t + SHORT_HALO, :]


def _short_conv_layer(x, g, w_in, conv_w, w_out):
    b, s, d = x.shape
    t = SEQ_TILE
    return pl.pallas_call(
        _short_conv_kernel,
        grid=(b, s // t),
        in_specs=[
            pl.BlockSpec((1, t, d), lambda bi, si: (bi, si, 0)),
            _resident((1, d)),
            _resident((d, 4 * D_INNER)),
            _resident((SHORT_KERNEL, D_INNER)),
            _resident((D_INNER, d)),
        ],
        out_specs=pl.BlockSpec((1, t, d), lambda bi, si: (bi, si, 0)),
        out_shape=jax.ShapeDtypeStruct(x.shape, F32),
        scratch_shapes=[
            pltpu.VMEM((t, d), BF16),
            pltpu.VMEM((t + SHORT_HALO, D_INNER), F32),
        ],
        compiler_params=_params(2),
        name="short_conv_layer",
    )(x, g.reshape(1, d), w_in.astype(BF16), conv_w, w_out.astype(BF16))


def _fox_proj_kernel(x_ref, g_ref, wqk_ref, wvt_ref, wz_ref, wf_hi_ref, wf_lo_ref, fb_ref,
                     qg_ref, kg_ref, qx_ref, kx_ref, vt_ref, gate_ref, h_ref, carry_ref):
    si = pl.program_id(1)
    t = SEQ_TILE
    dh = FOX_HEAD_DIM

    @pl.when(si == 0)
    def _():
        carry_ref[...] = jnp.zeros_like(carry_ref)

    x = x_ref[0]
    hf = _rms_norm_rows(x, g_ref[...])
    h_ref[...] = hf.astype(BF16)

    h_hi = h_ref[...]
    h_lo = (hf - h_hi.astype(F32)).astype(BF16)
    f_logit = (_dot(h_hi, wf_hi_ref[...]) + _dot(h_hi, wf_lo_ref[...])
               + _dot(h_lo, wf_hi_ref[...])) + fb_ref[...]
    log_f = jnp.minimum(f_logit, 0.0) - jnp.log1p(jnp.exp(-jnp.abs(f_logit)))
    rows = lax.broadcasted_iota(jnp.int32, (t, t), 0)
    cols = lax.broadcasted_iota(jnp.int32, (t, t), 1)
    tri = jnp.where(cols <= rows, 1.0, 0.0).astype(BF16)
    f_hi, f_mid, f_lo = _split3(log_f)
    c = (_dot(tri, f_hi) + _dot(tri, f_mid) + _dot(tri, f_lo)) + carry_ref[...]
    carry_ref[...] = c[t - 1:t, :]
    c_hi, c_mid, c_lo = _split3(c * LOG2_E)
    c_hi, c_mid, c_lo = c_hi.astype(F32), c_mid.astype(F32), c_lo.astype(F32)

    lane = lax.broadcasted_iota(jnp.int32, (t, dh), 1)
    scale = dh ** -0.5 * LOG2_E
    heads_per_chunk = CHAN_CHUNK // dh
    for c4 in range(D_INNER // CHAN_CHUNK):
        lo = c4 * CHAN_CHUNK
        q4 = _dot(h_ref[...], wqk_ref[:, lo:lo + CHAN_CHUNK])
        k4 = _dot(h_ref[...], wqk_ref[:, D_INNER + lo:D_INNER + lo + CHAN_CHUNK])
        for j in range(heads_per_chunk):
            hd = c4 * heads_per_chunk + j
            qh = _rms_norm_rows(q4[:, j * dh:(j + 1) * dh], qg_ref[...]) * scale
            kh = _rms_norm_rows(k4[:, j * dh:(j + 1) * dh], kg_ref[...])
            hi, mid, lw = c_hi[:, hd:hd + 1], c_mid[:, hd:hd + 1], c_lo[:, hd:hd + 1]
            q_ext = jnp.where(lane < 3, 1.0,
                              jnp.where(lane == 3, hi,
                                        jnp.where(lane == 4, mid,
                                                  jnp.where(lane == 5, lw, 0.0))))
            k_ext = jnp.where(lane == 0, -hi,
                              jnp.where(lane == 1, -mid,
                                        jnp.where(lane == 2, -lw,
                                                  jnp.where(lane < 6, 1.0, 0.0))))
            qx_ref[0, hd, :, 0:dh] = qh.astype(BF16)
            qx_ref[0, hd, :, dh:2 * dh] = q_ext.astype(BF16)
            kx_ref[0, hd, :, 0:dh] = kh.astype(BF16)
            kx_ref[0, hd, :, dh:2 * dh] = k_ext.astype(BF16)
        z = _dot(h_ref[...], wz_ref[:, lo:lo + CHAN_CHUNK])
        gate_ref[0, :, lo:lo + CHAN_CHUNK] = _silu(z).astype(BF16)
        vt = _dot_nt(wvt_ref[lo:lo + CHAN_CHUNK, :], h_ref[...])
        for j in range(heads_per_chunk):
            hd = c4 * heads_per_chunk + j
            vt_ref[0, hd, 0] = vt[j * dh:(j + 1) * dh, :].astype(BF16)


def _fox_proj(x, g, w_in, f_bias, q_norm, k_norm):
    b, s, d = x.shape
    t = SEQ_TILE
    h, dh = FOX_HEADS, FOX_HEAD_DIM
    e = D_INNER
    wqk = w_in[:, :2 * e].astype(BF16)
    wvt = w_in[:, 2 * e:3 * e].T.astype(BF16)
    wz = w_in[:, 3 * e:4 * e].astype(BF16)
    wf = w_in[:, 4 * e:]
    wf_hi = wf.astype(BF16)
    wf_lo = (wf - wf_hi.astype(F32)).astype(BF16)
    return pl.pallas_call(
        _fox_proj_kernel,
        grid=(b, s // t),
        in_specs=[
            pl.BlockSpec((1, t, d), lambda bi, si: (bi, si, 0)),
            _resident((1, d)),
            _resident((d, 2 * e)),
            _resident((e, d)),
            _resident((d, e)),
            _resident((d, h)), _resident((d, h)), _resident((1, h)),
            _resident((1, dh)), _resident((1, dh)),
        ],
        out_specs=[
            pl.BlockSpec((1, h, t, QK_WIDTH), lambda bi, si: (bi, 0, si, 0)),
            pl.BlockSpec((1, h, t, QK_WIDTH), lambda bi, si: (bi, 0, si, 0)),
            pl.BlockSpec((1, h, 1, dh, t), lambda bi, si: (bi, 0, si, 0, 0)),
            pl.BlockSpec((1, t, e), lambda bi, si: (bi, si, 0)),
        ],
        out_shape=[
            jax.ShapeDtypeStruct((b, h, s, QK_WIDTH), BF16),
            jax.ShapeDtypeStruct((b, h, s, QK_WIDTH), BF16),
            jax.ShapeDtypeStruct((b, h, s // t, dh, t), BF16),
            jax.ShapeDtypeStruct((b, s, e), BF16),
        ],
        scratch_shapes=[
            pltpu.VMEM((t, d), BF16),
            pltpu.VMEM((1, h), F32),
        ],
        compiler_params=_params(2),
        name="fox_proj",
    )(x, g.reshape(1, d), wqk, wvt, wz, wf_hi, wf_lo, f_bias.reshape(1, h),
      q_norm.reshape(1, dh), k_norm.reshape(1, dh))


def _fox_attn_kernel(qx_ref, kx_ref, vt_ref, gate_ref, y_ref,
                     s0_ref, s1_ref, p0_ref, p1_ref, a0_ref, a1_ref, m_ref, l_ref, acc_ref):
    qi = pl.program_id(2)
    kt = KV_TILE

    def scores(blk, s_ref, c0, c1):
        k0 = pl.multiple_of(blk * kt, kt)
        s_ref[:, c0:c1] = _dot_nt(kx_ref[0, 0, pl.ds(k0, kt), :], qx_ref[0, 0, c0:c1, :])

    def softmax(s_ref, p_ref, a_ref, c0, c1, masked):
        s = s_ref[:, c0:c1]
        if masked:
            kv_pos = lax.broadcasted_iota(jnp.int32, s.shape, 0)
            q_pos = lax.broadcasted_iota(jnp.int32, s.shape, 1)
            s = jnp.where(kv_pos <= q_pos, s, -jnp.inf)
        m_prev = m_ref[:, c0:c1]
        m_new = jnp.maximum(m_prev, jnp.max(s, axis=0, keepdims=True))
        p = jnp.exp2(s - m_new)
        alpha = jnp.exp2(m_prev - m_new)
        l_ref[:, c0:c1] = alpha * l_ref[:, c0:c1] + jnp.sum(p, axis=0, keepdims=True)
        m_ref[:, c0:c1] = m_new
        a_ref[:, c0:c1] = alpha
        p_ref[:, c0:c1] = p.astype(BF16)

    def accumulate(blk, p_ref, a_ref, c0, c1):
        acc_ref[:, c0:c1] = (a_ref[:, c0:c1] * acc_ref[:, c0:c1]
                             + _dot(vt_ref[0, 0, blk], p_ref[:, c0:c1]))

    m_ref[...] = jnp.full(m_ref.shape, -1e30, F32)
    l_ref[...] = jnp.zeros(l_ref.shape, F32)
    acc_ref[...] = jnp.zeros(acc_ref.shape, F32)
    p1_ref[...] = jnp.zeros(p1_ref.shape, BF16)
    a1_ref[...] = jnp.ones(a1_ref.shape, F32)

    full = (0, Q_TILE)
    scores(0, s0_ref, *full)

    def pair(t, carry):
        even = 2 * t
        scores(even + 1, s1_ref, *full)
        softmax(s0_ref, p0_ref, a0_ref, *full, masked=False)
        accumulate(jnp.maximum(even - 1, 0), p1_ref, a1_ref, *full)
        scores(even + 2, s0_ref, *full)
        softmax(s1_ref, p1_ref, a1_ref, *full, masked=False)
        accumulate(even, p0_ref, a0_ref, *full)
        return carry

    lax.fori_loop(0, qi, pair, 0)

    d0 = 2 * qi
    left, right = (0, kt), (kt, Q_TILE)
    scores(d0 + 1, s1_ref, *right)
    softmax(s0_ref, p0_ref, a0_ref, *left, masked=True)
    softmax(s0_ref, p0_ref, a0_ref, *right, masked=False)
    accumulate(jnp.maximum(d0 - 1, 0), p1_ref, a1_ref, *full)
    softmax(s1_ref, p1_ref, a1_ref, *right, masked=True)
    accumulate(d0, p0_ref, a0_ref, *full)
    accumulate(d0 + 1, p1_ref, a1_ref, *right)

    o_t = acc_ref[...] * (1.0 / l_ref[...])
    y_ref[0] = (o_t.T * gate_ref[0].astype(F32)).astype(BF16)


def _fox_attn_bounded_kernel(qx_ref, kx_ref, vt_ref, gate_ref, y_ref,
                             p0_ref, p1_ref, l_ref, acc_ref):
    qi = pl.program_id(2)
    kt = KV_TILE

    def probs(blk, p_ref, c0, c1, masked):
        k0 = pl.multiple_of(blk * kt, kt)
        s = _dot_nt(kx_ref[0, 0, pl.ds(k0, kt), :], qx_ref[0, 0, c0:c1, :])
        if masked:
            kv_pos = lax.broadcasted_iota(jnp.int32, s.shape, 0)
            q_pos = lax.broadcasted_iota(jnp.int32, s.shape, 1)
            s = jnp.where(kv_pos <= q_pos, s, -jnp.inf)
        p = jnp.exp2(s)
        l_ref[:, c0:c1] = l_ref[:, c0:c1] + jnp.sum(p, axis=0, keepdims=True)
        p_ref[:, c0:c1] = p.astype(BF16)

    def accumulate(blk, p_ref, c0, c1):
        acc_ref[:, c0:c1] = acc_ref[:, c0:c1] + _dot(vt_ref[0, 0, blk], p_ref[:, c0:c1])

    l_ref[...] = jnp.zeros(l_ref.shape, F32)
    acc_ref[...] = jnp.zeros(acc_ref.shape, F32)
    p1_ref[...] = jnp.zeros(p1_ref.shape, BF16)

    q_tile = p0_ref.shape[1]
    n_diag = q_tile // kt
    p_refs = (p0_ref, p1_ref)
    full = (0, q_tile)

    def pair(t, carry):
        even = 2 * t
        probs(even, p0_ref, *full, masked=False)
        accumulate(jnp.maximum(even - 1, 0), p1_ref, *full)
        probs(even + 1, p1_ref, *full, masked=False)
        accumulate(even, p0_ref, *full)
        return carry

    d0 = n_diag * qi
    lax.fori_loop(0, d0 // 2, pair, 0)

    def probs_diag(i):
        probs(d0 + i, p_refs[i % 2], i * kt, (i + 1) * kt, masked=True)
        if i + 1 < n_diag:
            probs(d0 + i, p_refs[i % 2], (i + 1) * kt, q_tile, masked=False)

    probs_diag(0)
    accumulate(jnp.maximum(d0 - 1, 0), p1_ref, *full)
    for i in range(1, n_diag):
        probs_diag(i)
        accumulate(d0 + i - 1, p_refs[(i - 1) % 2], (i - 1) * kt, q_tile)
    accumulate(d0 + n_diag - 1, p_refs[(n_diag - 1) % 2], (n_diag - 1) * kt, q_tile)

    o_t = acc_ref[...] * (1.0 / l_ref[...])
    y_ref[0] = (o_t.T * gate_ref[0].astype(F32)).astype(BF16)


def _fox_attention(qx, kx, vt, gate, *, bounded):
    b, h, s, _ = qx.shape
    dh = FOX_HEAD_DIM
    assert KV_TILE == SEQ_TILE and Q_TILE == 2 * KV_TILE and Q_TILE_BOUNDED % (2 * KV_TILE) == 0
    q_tile = Q_TILE_BOUNDED if bounded else Q_TILE
    probs = pltpu.VMEM((KV_TILE, q_tile), BF16)
    row = pltpu.VMEM((1, q_tile), F32)
    acc = pltpu.VMEM((dh, q_tile), F32)
    if bounded:
        body, scratch = _fox_attn_bounded_kernel, [probs, probs, row, acc]
    else:
        scores = pltpu.VMEM((KV_TILE, q_tile), F32)
        body, scratch = _fox_attn_kernel, [scores, scores, probs, probs, row, row, row, row, acc]
    return pl.pallas_call(
        body,
        grid=(b, h, s // q_tile),
        in_specs=[
            pl.BlockSpec((1, 1, q_tile, QK_WIDTH), lambda bi, hi, qi: (bi, hi, qi, 0)),
            pl.BlockSpec((1, 1, s, QK_WIDTH), lambda bi, hi, qi: (bi, hi, 0, 0)),
            pl.BlockSpec((1, 1, s // KV_TILE, dh, KV_TILE), lambda bi, hi, qi: (bi, hi, 0, 0, 0)),
            pl.BlockSpec((1, q_tile, dh), lambda bi, hi, qi: (bi, qi, hi)),
        ],
        out_specs=pl.BlockSpec((1, q_tile, dh), lambda bi, hi, qi: (bi, qi, hi)),
        out_shape=jax.ShapeDtypeStruct((b, s, h * dh), BF16),
        scratch_shapes=scratch,
        compiler_params=_params(3),
        name="fox_attention_bounded" if bounded else "fox_attention",
    )(qx, kx, vt, gate)


def _out_proj_kernel(x_ref, y_ref, w_ref, o_ref):
    o_ref[0] = x_ref[0] + _dot(y_ref[0], w_ref[...])


def _out_proj(x, y, w_out):
    b, s, d = x.shape
    t = SEQ_TILE
    e = y.shape[-1]
    return pl.pallas_call(
        _out_proj_kernel,
        grid=(b, s // t),
        in_specs=[
            pl.BlockSpec((1, t, d), lambda bi, si: (bi, si, 0)),
            pl.BlockSpec((1, t, e), lambda bi, si: (bi, si, 0)),
            _resident((e, d)),
        ],
        out_specs=pl.BlockSpec((1, t, d), lambda bi, si: (bi, si, 0)),
        out_shape=jax.ShapeDtypeStruct(x.shape, F32),
        compiler_params=_params(2),
        name="fox_out_proj",
    )(x, y, w_out.astype(BF16))


def _fox_layer(x, g, w_in, f_bias, q_norm, k_norm, w_out):
    qx, kx, vt, gate = _fox_proj(x, g, w_in, f_bias, q_norm, k_norm)
    qk_bound = (1.01 * FOX_HEAD_DIM ** 0.5 * LOG2_E) * jnp.max(jnp.abs(q_norm)) * jnp.max(jnp.abs(k_norm))
    y = lax.cond(qk_bound <= BOUNDED_LOGIT_LOG2,
                 functools.partial(_fox_attention, bounded=True),
                 functools.partial(_fox_attention, bounded=False),
                 qx, kx, vt, gate)
    return _out_proj(x, y, w_out)


def kernel(x, a_norm, a_w_in, a_conv_w, a_conv_b, a_ln_g, a_ln_b, a_w_out, b_norm, b_w_in, b_f_bias, b_q_norm, b_k_norm, b_w_out, c_norm, c_w_in, c_conv_w, c_w_out):
    depth = a_norm.shape[0] + b_norm.shape[0] + c_norm.shape[0]
    a_w_in_bf16, a_w_out_bf16 = a_w_in.astype(BF16), a_w_out.astype(BF16)
    for i in range(depth):
        kind, j = i % 3, i // 3
        if kind == 0:
            x = _conformer_layer(x, j, a_norm[j], a_w_in_bf16, a_conv_w[j], a_conv_b[j],
                                 a_ln_g[j], a_ln_b[j], a_w_out_bf16)
        elif kind == 1:
            x = _fox_layer(x, b_norm[j], b_w_in[j], b_f_bias[j], b_q_norm[j], b_k_norm[j],
                           b_w_out[j])
        else:
            x = _short_conv_layer(x, c_norm[j], c_w_in[j], c_conv_w[j], c_w_out[j])
    return x
```

```python
import functools

import jax
import jax.numpy as jnp
from jax import lax
from jax.experimental import pallas as pl
from jax.experimental.pallas import tpu as pltpu

D_MODEL = 1024
D_INNER = 2048
CONF_KERNEL = 31
SHORT_KERNEL = 3
FOX_HEADS = 16
FOX_HEAD_DIM = 128
NORM_EPS = 1e-6

V7X_LANES = 128
V7X_SUBLANES = 8
V7X_VMEM_LIMIT_BYTES = 58 * 1024 * 1024

SEQ_TILE = 512
CHAN_CHUNK = 512
CONF_HALO = 32
SHORT_HALO = 8
CONV_ROWS = 64
KV_TILE = 512
Q_TILE = 2 * KV_TILE
Q_TILE_BOUNDED = 8 * KV_TILE
QK_WIDTH = 2 * FOX_HEAD_DIM
LOG2_E = 1.4426950408889634
BOUNDED_LOGIT_LOG2 = 50.0

BF16 = jnp.bfloat16
F32 = jnp.float32


def _sigmoid(x):
    return 1.0 / (1.0 + jnp.exp(-x))


def _silu(x):
    return x * _sigmoid(x)


def _rms_norm_rows(x, g):
    ms = jnp.mean(x * x, axis=-1, keepdims=True)
    return x * lax.rsqrt(ms + NORM_EPS) * g


def _dot(a, b):
    return jnp.dot(a, b, preferred_element_type=F32)


def _dot_nt(a, b):
    return lax.dot_general(a, b, (((1,), (1,)), ((), ())), preferred_element_type=F32)


def _split3(x):
    hi = x.astype(BF16)
    r = x - hi.astype(F32)
    mid = r.astype(BF16)
    lo = (r - mid.astype(F32)).astype(BF16)
    return hi, mid, lo


def _resident(shape):
    zeros = (0,) * len(shape)
    return pl.BlockSpec(shape, lambda *_: zeros, pipeline_mode=pl.Buffered(1))


def _resident_layer(shape, j):
    zeros = (0,) * len(shape)
    return pl.BlockSpec((None,) + tuple(shape), lambda *_: (j,) + zeros,
                        pipeline_mode=pl.Buffered(1))


def _params(n_axes):
    return pltpu.CompilerParams(
        dimension_semantics=("arbitrary",) * n_axes,
        vmem_limit_bytes=V7X_VMEM_LIMIT_BYTES,
    )


def _conformer_kernel(x_ref, g_ref, w_in_ref, cw_ref, cb_ref, lg_ref, lb_ref, w_out_ref,
                      o_ref, h_ref, u_ref, c_ref):
    si = pl.program_id(1)
    t = SEQ_TILE
    n_chunks = D_INNER // CHAN_CHUNK

    n_slabs = D_INNER // V7X_LANES
    slabs_per_chunk = CHAN_CHUNK // V7X_LANES

    @pl.when(si == 0)
    def _():
        u_ref[:, 0:CONF_HALO, :] = jnp.zeros((n_slabs, CONF_HALO, V7X_LANES), F32)

    x = x_ref[0]
    h_ref[...] = _rms_norm_rows(x, g_ref[...]).astype(BF16)

    for c in range(n_chunks):
        lo = c * CHAN_CHUNK
        val = _dot(h_ref[...], w_in_ref[:, lo:lo + CHAN_CHUNK])
        gate = _dot(h_ref[...], w_in_ref[:, D_INNER + lo:D_INNER + lo + CHAN_CHUNK])
        u = val * _sigmoid(gate)
        for j in range(slabs_per_chunk):
            u_ref[c * slabs_per_chunk + j, CONF_HALO:CONF_HALO + t, :] = (
                u[:, j * V7X_LANES:(j + 1) * V7X_LANES])

    first = CONF_HALO - (CONF_KERNEL - 1)

    def conv_slab(sl, carry):
        for rb in range(t // CONV_ROWS):
            r0 = rb * CONV_ROWS
            acc = jnp.zeros((CONV_ROWS, V7X_LANES), F32)
            for k in range(CONF_KERNEL):
                acc = acc + (cw_ref[sl, k:k + 1, :]
                             * u_ref[sl, r0 + first + k:r0 + first + k + CONV_ROWS, :])
            c_ref[sl, r0:r0 + CONV_ROWS, :] = acc + cb_ref[sl]
        return carry

    lax.fori_loop(0, n_slabs, conv_slab, 0)

    u_ref[:, 0:CONF_HALO, :] = u_ref[:, t:t + CONF_HALO, :]

    row_sum = c_ref[0]
    for sl in range(1, n_slabs):
        row_sum = row_sum + c_ref[sl]
    mu = jnp.sum(row_sum, axis=-1, keepdims=True) * (1.0 / D_INNER)
    sq_sum = jnp.square(c_ref[0] - mu)
    for sl in range(1, n_slabs):
        sq_sum = sq_sum + jnp.square(c_ref[sl] - mu)
    var = jnp.sum(sq_sum, axis=-1, keepdims=True) * (1.0 / D_INNER)
    rstd = lax.rsqrt(var + NORM_EPS)

    acc = x
    for c in range(n_chunks):
        lo = c * CHAN_CHUNK
        z = _dot(h_ref[...], w_in_ref[:, 2 * D_INNER + lo:2 * D_INNER + lo + CHAN_CHUNK])
        conv = jnp.concatenate(
            [c_ref[c * slabs_per_chunk + j] for j in range(slabs_per_chunk)], axis=-1)
        ln = (conv - mu) * rstd * lg_ref[:, lo:lo + CHAN_CHUNK] + lb_ref[:, lo:lo + CHAN_CHUNK]
        y = (_silu(ln) * _silu(z)).astype(BF16)
        acc = acc + _dot(y, w_out_ref[lo:lo + CHAN_CHUNK, :])
    o_ref[0] = acc


def _conformer_layer(x, j, g, w_in_stack, conv_w, conv_b, ln_g, ln_b, w_out_stack):
    b, s, d = x.shape
    t = SEQ_TILE
    row = lambda n: _resident((1, n))
    n_slabs = D_INNER // V7X_LANES
    cw = conv_w.reshape(CONF_KERNEL, n_slabs, V7X_LANES).transpose(1, 0, 2)
    cb = conv_b.reshape(n_slabs, 1, V7X_LANES)
    return pl.pallas_call(
        _conformer_kernel,
        grid=(b, s // t),
        in_specs=[
            pl.BlockSpec((1, t, d), lambda bi, si: (bi, si, 0)),
            row(d),
            _resident_layer((d, 3 * D_INNER), j),
            _resident((n_slabs, CONF_KERNEL, V7X_LANES)),
            _resident((n_slabs, 1, V7X_LANES)),
            row(D_INNER), row(D_INNER),
            _resident_layer((D_INNER, d), j),
        ],
        out_specs=pl.BlockSpec((1, t, d), lambda bi, si: (bi, si, 0)),
        out_shape=jax.ShapeDtypeStruct(x.shape, F32),
        scratch_shapes=[
            pltpu.VMEM((t, d), BF16),
            pltpu.VMEM((n_slabs, t + CONF_HALO, V7X_LANES), F32),
            pltpu.VMEM((n_slabs, t, V7X_LANES), F32),
        ],
        compiler_params=_params(2),
        name="conformer_layer",
    )(x, g.reshape(1, d), w_in_stack, cw, cb,
      ln_g.reshape(1, -1), ln_b.reshape(1, -1), w_out_stack)


def _short_conv_kernel(x_ref, g_ref, w_in_ref, cw_ref, w_out_ref, o_ref, h_ref, u_ref):
    si = pl.program_id(1)
    t = SEQ_TILE
    n_chunks = D_INNER // CHAN_CHUNK

    @pl.when(si == 0)
    def _():
        u_ref[0:SHORT_HALO, :] = jnp.zeros((SHORT_HALO, D_INNER), F32)

    x = x_ref[0]
    h_ref[...] = _rms_norm_rows(x, g_ref[...]).astype(BF16)

    acc = x
    for c in range(n_chunks):
        lo = c * CHAN_CHUNK
        cols = lambda part: slice(part * D_INNER + lo, part * D_INNER + lo + CHAN_CHUNK)
        u = _dot(h_ref[...], w_in_ref[:, cols(0)])
        c_gate = _dot(h_ref[...], w_in_ref[:, cols(2)])
        u_ref[SHORT_HALO:SHORT_HALO + t, lo:lo + CHAN_CHUNK] = c_gate * u
        conv = jnp.zeros((t, CHAN_CHUNK), F32)
        for k in range(SHORT_KERNEL):
            first = SHORT_HALO - (SHORT_KERNEL - 1) + k
            conv = conv + cw_ref[k:k + 1, lo:lo + CHAN_CHUNK] * u_ref[first:first + t, lo:lo + CHAN_CHUNK]
        b_gate = _dot(h_ref[...], w_in_ref[:, cols(1)])
        z = _dot(h_ref[...], w_in_ref[:, cols(3)])
        y = (b_gate * conv * _silu(z)).astype(BF16)
        acc = acc + _dot(y, w_out_ref[lo:lo + CHAN_CHUNK, :])
    o_ref[0] = acc

    u_ref[0:SHORT_HALO, :] = u_ref[t:t + SHORT_HALO, :]


def _short_conv_layer(x, g, w_in, conv_w, w_out):
    b, s, d = x.shape
    t = SEQ_TILE
    return pl.pallas_call(
        _short_conv_kernel,
        grid=(b, s // t),
        in_specs=[
            pl.BlockSpec((1, t, d), lambda bi, si: (bi, si, 0)),
            _resident((1, d)),
            _resident((d, 4 * D_INNER)),
            _resident((SHORT_KERNEL, D_INNER)),
            _resident((D_INNER, d)),
        ],
        out_specs=pl.BlockSpec((1, t, d), lambda bi, si: (bi, si, 0)),
        out_shape=jax.ShapeDtypeStruct(x.shape, F32),
        scratch_shapes=[
            pltpu.VMEM((t, d), BF16),
            pltpu.VMEM((t + SHORT_HALO, D_INNER), F32),
        ],
        compiler_params=_params(2),
        name="short_conv_layer",
    )(x, g.reshape(1, d), w_in.astype(BF16), conv_w, w_out.astype(BF16))


def _fox_proj_kernel(x_ref, g_ref, wt_ref, wf_hi_ref, wf_lo_ref, fb_ref,
                     qg_ref, kg_ref, qx_ref, kx_ref, vt_ref, gate_ref, h_ref, carry_ref):
    si = pl.program_id(1)
    t = SEQ_TILE
    dh = FOX_HEAD_DIM

    @pl.when(si == 0)
    def _():
        carry_ref[...] = jnp.zeros_like(carry_ref)

    x = x_ref[0]
    hf = _rms_norm_rows(x, g_ref[...])
    h_ref[...] = hf.astype(BF16)

    h_hi = h_ref[...]
    h_lo = (hf - h_hi.astype(F32)).astype(BF16)
    f_logit = (_dot_nt(h_hi, wf_hi_ref[...]) + _dot_nt(h_hi, wf_lo_ref[...])
               + _dot_nt(h_lo, wf_hi_ref[...])) + fb_ref[...]
    log_f = jnp.minimum(f_logit, 0.0) - jnp.log1p(jnp.exp(-jnp.abs(f_logit)))
    rows = lax.broadcasted_iota(jnp.int32, (t, t), 0)
    cols = lax.broadcasted_iota(jnp.int32, (t, t), 1)
    tri = jnp.where(cols <= rows, 1.0, 0.0).astype(BF16)
    f_hi, f_mid, f_lo = _split3(log_f)
    c = (_dot(tri, f_hi) + _dot(tri, f_mid) + _dot(tri, f_lo)) + carry_ref[...]
    carry_ref[...] = c[t - 1:t, :]
    c_hi, c_mid, c_lo = _split3(c * LOG2_E)
    c_hi, c_mid, c_lo = c_hi.astype(F32), c_mid.astype(F32), c_lo.astype(F32)

    lane = lax.broadcasted_iota(jnp.int32, (t, dh), 1)
    scale = dh ** -0.5 * LOG2_E
    heads_per_chunk = CHAN_CHUNK // dh
    for c4 in range(D_INNER // CHAN_CHUNK):
        lo = c4 * CHAN_CHUNK
        q4 = _dot_nt(h_ref[...], wt_ref[lo:lo + CHAN_CHUNK, :])
        k4 = _dot_nt(h_ref[...], wt_ref[D_INNER + lo:D_INNER + lo + CHAN_CHUNK, :])
        for j in range(heads_per_chunk):
            hd = c4 * heads_per_chunk + j
            qh = _rms_norm_rows(q4[:, j * dh:(j + 1) * dh], qg_ref[...]) * scale
            kh = _rms_norm_rows(k4[:, j * dh:(j + 1) * dh], kg_ref[...])
            hi, mid, lw = c_hi[:, hd:hd + 1], c_mid[:, hd:hd + 1], c_lo[:, hd:hd + 1]
            q_ext = jnp.where(lane < 3, 1.0,
                              jnp.where(lane == 3, hi,
                                        jnp.where(lane == 4, mid,
                                                  jnp.where(lane == 5, lw, 0.0))))
            k_ext = jnp.where(lane == 0, -hi,
                              jnp.where(lane == 1, -mid,
                                        jnp.where(lane == 2, -lw,
                                                  jnp.where(lane < 6, 1.0, 0.0))))
            qx_ref[0, hd, :, 0:dh] = qh.astype(BF16)
            qx_ref[0, hd, :, dh:2 * dh] = q_ext.astype(BF16)
            kx_ref[0, hd, :, 0:dh] = kh.astype(BF16)
            kx_ref[0, hd, :, dh:2 * dh] = k_ext.astype(BF16)
        z = _dot_nt(h_ref[...], wt_ref[3 * D_INNER + lo:3 * D_INNER + lo + CHAN_CHUNK, :])
        gate_ref[0, :, lo:lo + CHAN_CHUNK] = _silu(z).astype(BF16)
        vt = _dot_nt(wt_ref[2 * D_INNER + lo:2 * D_INNER + lo + CHAN_CHUNK, :], h_ref[...])
        for j in range(heads_per_chunk):
            hd = c4 * heads_per_chunk + j
            vt_ref[0, hd, 0] = vt[j * dh:(j + 1) * dh, :].astype(BF16)


def _fox_proj(x, g, w_in, f_bias, q_norm, k_norm):
    b, s, d = x.shape
    t = SEQ_TILE
    h, dh = FOX_HEADS, FOX_HEAD_DIM
    e = D_INNER
    wt_f32 = w_in.T
    wt = wt_f32.astype(BF16)
    wf = wt_f32[4 * e:]
    wf_hi = wf.astype(BF16)
    wf_lo = (wf - wf_hi.astype(F32)).astype(BF16)
    return pl.pallas_call(
        _fox_proj_kernel,
        grid=(b, s // t),
        in_specs=[
            pl.BlockSpec((1, t, d), lambda bi, si: (bi, si, 0)),
            _resident((1, d)),
            _resident((4 * e + h, d)),
            _resident((h, d)), _resident((h, d)), _resident((1, h)),
            _resident((1, dh)), _resident((1, dh)),
        ],
        out_specs=[
            pl.BlockSpec((1, h, t, QK_WIDTH), lambda bi, si: (bi, 0, si, 0)),
            pl.BlockSpec((1, h, t, QK_WIDTH), lambda bi, si: (bi, 0, si, 0)),
            pl.BlockSpec((1, h, 1, dh, t), lambda bi, si: (bi, 0, si, 0, 0)),
            pl.BlockSpec((1, t, e), lambda bi, si: (bi, si, 0)),
        ],
        out_shape=[
            jax.ShapeDtypeStruct((b, h, s, QK_WIDTH), BF16),
            jax.ShapeDtypeStruct((b, h, s, QK_WIDTH), BF16),
            jax.ShapeDtypeStruct((b, h, s // t, dh, t), BF16),
            jax.ShapeDtypeStruct((b, s, e), BF16),
        ],
        scratch_shapes=[
            pltpu.VMEM((t, d), BF16),
            pltpu.VMEM((1, h), F32),
        ],
        compiler_params=_params(2),
        name="fox_proj",
    )(x, g.reshape(1, d), wt, wf_hi, wf_lo, f_bias.reshape(1, h),
      q_norm.reshape(1, dh), k_norm.reshape(1, dh))


def _fox_attn_kernel(qx_ref, kx_ref, vt_ref, gate_ref, y_ref,
                     s0_ref, s1_ref, p0_ref, p1_ref, a0_ref, a1_ref, m_ref, l_ref, acc_ref):
    qi = pl.program_id(2)
    kt = KV_TILE

    def scores(blk, s_ref, c0, c1):
        k0 = pl.multiple_of(blk * kt, kt)
        s_ref[:, c0:c1] = _dot_nt(kx_ref[0, 0, pl.ds(k0, kt), :], qx_ref[0, 0, c0:c1, :])

    def softmax(s_ref, p_ref, a_ref, c0, c1, masked):
        s = s_ref[:, c0:c1]
        if masked:
            kv_pos = lax.broadcasted_iota(jnp.int32, s.shape, 0)
            q_pos = lax.broadcasted_iota(jnp.int32, s.shape, 1)
            s = jnp.where(kv_pos <= q_pos, s, -jnp.inf)
        m_prev = m_ref[:, c0:c1]
        m_new = jnp.maximum(m_prev, jnp.max(s, axis=0, keepdims=True))
        p = jnp.exp2(s - m_new)
        alpha = jnp.exp2(m_prev - m_new)
        l_ref[:, c0:c1] = alpha * l_ref[:, c0:c1] + jnp.sum(p, axis=0, keepdims=True)
        m_ref[:, c0:c1] = m_new
        a_ref[:, c0:c1] = alpha
        p_ref[:, c0:c1] = p.astype(BF16)

    def accumulate(blk, p_ref, a_ref, c0, c1):
        acc_ref[:, c0:c1] = (a_ref[:, c0:c1] * acc_ref[:, c0:c1]
                             + _dot(vt_ref[0, 0, blk], p_ref[:, c0:c1]))

    m_ref[...] = jnp.full(m_ref.shape, -1e30, F32)
    l_ref[...] = jnp.zeros(l_ref.shape, F32)
    acc_ref[...] = jnp.zeros(acc_ref.shape, F32)
    p1_ref[...] = jnp.zeros(p1_ref.shape, BF16)
    a1_ref[...] = jnp.ones(a1_ref.shape, F32)

    full = (0, Q_TILE)
    scores(0, s0_ref, *full)

    def pair(t, carry):
        even = 2 * t
        scores(even + 1, s1_ref, *full)
        softmax(s0_ref, p0_ref, a0_ref, *full, masked=False)
        accumulate(jnp.maximum(even - 1, 0), p1_ref, a1_ref, *full)
        scores(even + 2, s0_ref, *full)
        softmax(s1_ref, p1_ref, a1_ref, *full, masked=False)
        accumulate(even, p0_ref, a0_ref, *full)
        return carry

    lax.fori_loop(0, qi, pair, 0)

    d0 = 2 * qi
    left, right = (0, kt), (kt, Q_TILE)
    scores(d0 + 1, s1_ref, *right)
    softmax(s0_ref, p0_ref, a0_ref, *left, masked=True)
    softmax(s0_ref, p0_ref, a0_ref, *right, masked=False)
    accumulate(jnp.maximum(d0 - 1, 0), p1_ref, a1_ref, *full)
    softmax(s1_ref, p1_ref, a1_ref, *right, masked=True)
    accumulate(d0, p0_ref, a0_ref, *full)
    accumulate(d0 + 1, p1_ref, a1_ref, *right)

    o_t = acc_ref[...] * (1.0 / l_ref[...])
    y_ref[0] = (o_t.T * gate_ref[0].astype(F32)).astype(BF16)


def _fox_attn_bounded_kernel(qx_ref, kx_ref, vt_ref, gate_ref, y_ref,
                             p0_ref, p1_ref, l_ref, acc_ref):
    qi = pl.program_id(2)
    kt = KV_TILE

    def probs(blk, p_ref, c0, c1, masked):
        k0 = pl.multiple_of(blk * kt, kt)
        s = _dot_nt(kx_ref[0, 0, pl.ds(k0, kt), :], qx_ref[0, 0, c0:c1, :])
        if masked:
            kv_pos = lax.broadcasted_iota(jnp.int32, s.shape, 0)
            q_pos = lax.broadcasted_iota(jnp.int32, s.shape, 1)
            s = jnp.where(kv_pos <= q_pos, s, -jnp.inf)
        p = jnp.exp2(s)
        l_ref[:, c0:c1] = l_ref[:, c0:c1] + jnp.sum(p, axis=0, keepdims=True)
        p_ref[:, c0:c1] = p.astype(BF16)

    def accumulate(blk, p_ref, c0, c1):
        acc_ref[:, c0:c1] = acc_ref[:, c0:c1] + _dot(vt_ref[0, 0, blk], p_ref[:, c0:c1])

    l_ref[...] = jnp.zeros(l_ref.shape, F32)
    acc_ref[...] = jnp.zeros(acc_ref.shape, F32)
    p1_ref[...] = jnp.zeros(p1_ref.shape, BF16)

    q_tile = p0_ref.shape[1]
    n_diag = q_tile // kt
    p_refs = (p0_ref, p1_ref)
    full = (0, q_tile)

    def pair(t, carry):
        even = 2 * t
        probs(even, p0_ref, *full, masked=False)
        accumulate(jnp.maximum(even - 1, 0), p1_ref, *full)
        probs(even + 1, p1_ref, *full, masked=False)
        accumulate(even, p0_ref, *full)
        return carry

    d0 = n_diag * qi
    lax.fori_loop(0, d0 // 2, pair, 0)

    def probs_diag(i):
        probs(d0 + i, p_refs[i % 2], i * kt, (i + 1) * kt, masked=True)
        if i + 1 < n_diag:
            probs(d0 + i, p_refs[i % 2], (i + 1) * kt, q_tile, masked=False)

    probs_diag(0)
    accumulate(jnp.maximum(d0 - 1, 0), p1_ref, *full)
    for i in range(1, n_diag):
        probs_diag(i)
        accumulate(d0 + i - 1, p_refs[(i - 1) % 2], (i - 1) * kt, q_tile)
    accumulate(d0 + n_diag - 1, p_refs[(n_diag - 1) % 2], (n_diag - 1) * kt, q_tile)

    o_t = acc_ref[...] * (1.0 / l_ref[...])
    y_ref[0] = (o_t.T * gate_ref[0].astype(F32)).astype(BF16)


def _fox_attention(qx, kx, vt, gate, *, bounded):
    b, h, s, _ = qx.shape
    dh = FOX_HEAD_DIM
    assert KV_TILE == SEQ_TILE and Q_TILE == 2 * KV_TILE and Q_TILE_BOUNDED % (2 * KV_TILE) == 0
    q_tile = Q_TILE_BOUNDED if bounded else Q_TILE
    probs = pltpu.VMEM((KV_TILE, q_tile), BF16)
    row = pltpu.VMEM((1, q_tile), F32)
    acc = pltpu.VMEM((dh, q_tile), F32)
    if bounded:
        body, scratch = _fox_attn_bounded_kernel, [probs, probs, row, acc]
    else:
        scores = pltpu.VMEM((KV_TILE, q_tile), F32)
        body, scratch = _fox_attn_kernel, [scores, scores, probs, probs, row, row, row, row, acc]
    return pl.pallas_call(
        body,
        grid=(b, h, s // q_tile),
        in_specs=[
            pl.BlockSpec((1, 1, q_tile, QK_WIDTH), lambda bi, hi, qi: (bi, hi, qi, 0)),
            pl.BlockSpec((1, 1, s, QK_WIDTH), lambda bi, hi, qi: (bi, hi, 0, 0)),
            pl.BlockSpec((1, 1, s // KV_TILE, dh, KV_TILE), lambda bi, hi, qi: (bi, hi, 0, 0, 0)),
            pl.BlockSpec((1, q_tile, dh), lambda bi, hi, qi: (bi, qi, hi)),
        ],
        out_specs=pl.BlockSpec((1, q_tile, dh), lambda bi, hi, qi: (bi, qi, hi)),
        out_shape=jax.ShapeDtypeStruct((b, s, h * dh), BF16),
        scratch_shapes=scratch,
        compiler_params=_params(3),
        name="fox_attention_bounded" if bounded else "fox_attention",
    )(qx, kx, vt, gate)


def _out_proj_kernel(x_ref, y_ref, w_ref, o_ref):
    o_ref[0] = x_ref[0] + _dot(y_ref[0], w_ref[...])


def _out_proj(x, y, w_out):
    b, s, d = x.shape
    t = SEQ_TILE
    e = y.shape[-1]
    return pl.pallas_call(
        _out_proj_kernel,
        grid=(b, s // t),
        in_specs=[
            pl.BlockSpec((1, t, d), lambda bi, si: (bi, si, 0)),
            pl.BlockSpec((1, t, e), lambda bi, si: (bi, si, 0)),
            _resident((e, d)),
        ],
        out_specs=pl.BlockSpec((1, t, d), lambda bi, si: (bi, si, 0)),
        out_shape=jax.ShapeDtypeStruct(x.shape, F32),
        compiler_params=_params(2),
        name="fox_out_proj",
    )(x, y, w_out.astype(BF16))


def _fox_layer(x, g, w_in, f_bias, q_norm, k_norm, w_out):
    qx, kx, vt, gate = _fox_proj(x, g, w_in, f_bias, q_norm, k_norm)
    qk_bound = (1.01 * FOX_HEAD_DIM ** 0.5 * LOG2_E) * jnp.max(jnp.abs(q_norm)) * jnp.max(jnp.abs(k_norm))
    y = lax.cond(qk_bound <= BOUNDED_LOGIT_LOG2,
                 functools.partial(_fox_attention, bounded=True),
                 functools.partial(_fox_attention, bounded=False),
                 qx, kx, vt, gate)
    return _out_proj(x, y, w_out)


def kernel(x, a_norm, a_w_in, a_conv_w, a_conv_b, a_ln_g, a_ln_b, a_w_out, b_norm, b_w_in, b_f_bias, b_q_norm, b_k_norm, b_w_out, c_norm, c_w_in, c_conv_w, c_w_out):
    depth = a_norm.shape[0] + b_norm.shape[0] + c_norm.shape[0]
    a_w_in_bf16, a_w_out_bf16 = a_w_in.astype(BF16), a_w_out.astype(BF16)
    for i in range(depth):
        kind, j = i % 3, i // 3
        if kind == 0:
            x = _conformer_layer(x, j, a_norm[j], a_w_in_bf16, a_conv_w[j], a_conv_b[j],
                                 a_ln_g[j], a_ln_b[j], a_w_out_bf16)
        elif kind == 1:
            x = _fox_layer(x, b_norm[j], b_w_in[j], b_f_bias[j], b_q_norm[j], b_k_norm[j],
                           b_w_out[j])
        else:
            x = _short_conv_layer(x, c_norm[j], c_w_in[j], c_conv_w[j], c_w_out[j])
    return x
```

```python
import functools

import jax
import jax.numpy as jnp
from jax import lax
from jax.experimental import pallas as pl
from jax.experimental.pallas import tpu as pltpu

D_MODEL = 1024
D_INNER = 2048
CONF_KERNEL = 31
SHORT_KERNEL = 3
FOX_HEADS = 16
FOX_HEAD_DIM = 128
NORM_EPS = 1e-6

V7X_LANES = 128
V7X_SUBLANES = 8
V7X_VMEM_LIMIT_BYTES = 58 * 1024 * 1024

SEQ_TILE = 512
CHAN_CHUNK = 512
CONF_HALO = 32
SHORT_HALO = 8
CONV_ROWS = 64
KV_TILE = 512
Q_TILE = 2 * KV_TILE
Q_TILE_BOUNDED = 8 * KV_TILE
QK_WIDTH = 2 * FOX_HEAD_DIM
LOG2_E = 1.4426950408889634
BOUNDED_LOGIT_LOG2 = 50.0

BF16 = jnp.bfloat16
F32 = jnp.float32


def _sigmoid(x):
    return 1.0 / (1.0 + jnp.exp(-x))


def _silu(x):
    return x * _sigmoid(x)


def _rms_norm_rows(x, g):
    ms = jnp.mean(x * x, axis=-1, keepdims=True)
    return x * lax.rsqrt(ms + NORM_EPS) * g


def _dot(a, b):
    return jnp.dot(a, b, preferred_element_type=F32)


def _dot_nt(a, b):
    return lax.dot_general(a, b, (((1,), (1,)), ((), ())), preferred_element_type=F32)


def _split3(x):
    hi = x.astype(BF16)
    r = x - hi.astype(F32)
    mid = r.astype(BF16)
    lo = (r - mid.astype(F32)).astype(BF16)
    return hi, mid, lo


def _resident(shape):
    zeros = (0,) * len(shape)
    return pl.BlockSpec(shape, lambda *_: zeros, pipeline_mode=pl.Buffered(1))


def _resident_layer(shape, j):
    zeros = (0,) * len(shape)
    return pl.BlockSpec((None,) + tuple(shape), lambda *_: (j,) + zeros,
                        pipeline_mode=pl.Buffered(1))


def _params(n_axes, n_inputs=None, fused_inputs=()):
    fusion = None
    if fused_inputs:
        fusion = [i in fused_inputs for i in range(n_inputs)]
    return pltpu.CompilerParams(
        dimension_semantics=("arbitrary",) * n_axes,
        vmem_limit_bytes=V7X_VMEM_LIMIT_BYTES,
        allow_input_fusion=fusion,
    )


def _conformer_kernel(x_ref, g_ref, w_in_ref, cw_ref, cb_ref, lg_ref, lb_ref, w_out_ref,
                      o_ref, h_ref, u_ref, c_ref):
    si = pl.program_id(1)
    t = SEQ_TILE
    n_chunks = D_INNER // CHAN_CHUNK

    n_slabs = D_INNER // V7X_LANES
    slabs_per_chunk = CHAN_CHUNK // V7X_LANES

    @pl.when(si == 0)
    def _():
        u_ref[:, 0:CONF_HALO, :] = jnp.zeros((n_slabs, CONF_HALO, V7X_LANES), F32)

    x = x_ref[0]
    h_ref[...] = _rms_norm_rows(x, g_ref[...]).astype(BF16)

    for c in range(n_chunks):
        lo = c * CHAN_CHUNK
        val = _dot(h_ref[...], w_in_ref[:, lo:lo + CHAN_CHUNK])
        gate = _dot(h_ref[...], w_in_ref[:, D_INNER + lo:D_INNER + lo + CHAN_CHUNK])
        u = val * _sigmoid(gate)
        for j in range(slabs_per_chunk):
            u_ref[c * slabs_per_chunk + j, CONF_HALO:CONF_HALO + t, :] = (
                u[:, j * V7X_LANES:(j + 1) * V7X_LANES])

    first = CONF_HALO - (CONF_KERNEL - 1)

    def conv_slab(sl, carry):
        for rb in range(t // CONV_ROWS):
            r0 = rb * CONV_ROWS
            acc = jnp.zeros((CONV_ROWS, V7X_LANES), F32)
            for k in range(CONF_KERNEL):
                acc = acc + (cw_ref[sl, k:k + 1, :]
                             * u_ref[sl, r0 + first + k:r0 + first + k + CONV_ROWS, :])
            c_ref[sl, r0:r0 + CONV_ROWS, :] = acc + cb_ref[sl]
        return carry

    lax.fori_loop(0, n_slabs, conv_slab, 0)

    u_ref[:, 0:CONF_HALO, :] = u_ref[:, t:t + CONF_HALO, :]

    row_sum = c_ref[0]
    for sl in range(1, n_slabs):
        row_sum = row_sum + c_ref[sl]
    mu = jnp.sum(row_sum, axis=-1, keepdims=True) * (1.0 / D_INNER)
    sq_sum = jnp.square(c_ref[0] - mu)
    for sl in range(1, n_slabs):
        sq_sum = sq_sum + jnp.square(c_ref[sl] - mu)
    var = jnp.sum(sq_sum, axis=-1, keepdims=True) * (1.0 / D_INNER)
    rstd = lax.rsqrt(var + NORM_EPS)

    acc = x
    for c in range(n_chunks):
        lo = c * CHAN_CHUNK
        z = _dot(h_ref[...], w_in_ref[:, 2 * D_INNER + lo:2 * D_INNER + lo + CHAN_CHUNK])
        conv = jnp.concatenate(
            [c_ref[c * slabs_per_chunk + j] for j in range(slabs_per_chunk)], axis=-1)
        ln = (conv - mu) * rstd * lg_ref[:, lo:lo + CHAN_CHUNK] + lb_ref[:, lo:lo + CHAN_CHUNK]
        y = (_silu(ln) * _silu(z)).astype(BF16)
        acc = acc + _dot(y, w_out_ref[lo:lo + CHAN_CHUNK, :])
    o_ref[0] = acc


def _conformer_layer(x, j, g, w_in_stack, conv_w, conv_b, ln_g, ln_b, w_out_stack):
    b, s, d = x.shape
    t = SEQ_TILE
    row = lambda n: _resident((1, n))
    n_slabs = D_INNER // V7X_LANES
    cw = conv_w.reshape(CONF_KERNEL, n_slabs, V7X_LANES).transpose(1, 0, 2)
    cb = conv_b.reshape(n_slabs, 1, V7X_LANES)
    return pl.pallas_call(
        _conformer_kernel,
        grid=(b, s // t),
        in_specs=[
            pl.BlockSpec((1, t, d), lambda bi, si: (bi, si, 0)),
            row(d),
            _resident_layer((d, 3 * D_INNER), j),
            _resident((n_slabs, CONF_KERNEL, V7X_LANES)),
            _resident((n_slabs, 1, V7X_LANES)),
            row(D_INNER), row(D_INNER),
            _resident_layer((D_INNER, d), j),
        ],
        out_specs=pl.BlockSpec((1, t, d), lambda bi, si: (bi, si, 0)),
        out_shape=jax.ShapeDtypeStruct(x.shape, F32),
        scratch_shapes=[
            pltpu.VMEM((t, d), BF16),
            pltpu.VMEM((n_slabs, t + CONF_HALO, V7X_LANES), F32),
            pltpu.VMEM((n_slabs, t, V7X_LANES), F32),
        ],
        compiler_params=_params(2),
        name="conformer_layer",
    )(x, g.reshape(1, d), w_in_stack, cw, cb,
      ln_g.reshape(1, -1), ln_b.reshape(1, -1), w_out_stack)


def _short_conv_kernel(x_ref, g_ref, w_in_ref, cw_ref, w_out_ref, o_ref, h_ref, u_ref):
    si = pl.program_id(1)
    t = SEQ_TILE
    n_chunks = D_INNER // CHAN_CHUNK

    @pl.when(si == 0)
    def _():
        u_ref[0:SHORT_HALO, :] = jnp.zeros((SHORT_HALO, D_INNER), F32)

    x = x_ref[0]
    h_ref[...] = _rms_norm_rows(x, g_ref[...]).astype(BF16)

    acc = x
    for c in range(n_chunks):
        lo = c * CHAN_CHUNK
        cols = lambda part: slice(part * D_INNER + lo, part * D_INNER + lo + CHAN_CHUNK)
        u = _dot(h_ref[...], w_in_ref[:, cols(0)])
        c_gate = _dot(h_ref[...], w_in_ref[:, cols(2)])
        u_ref[SHORT_HALO:SHORT_HALO + t, lo:lo + CHAN_CHUNK] = c_gate * u
        conv = jnp.zeros((t, CHAN_CHUNK), F32)
        for k in range(SHORT_KERNEL):
            first = SHORT_HALO - (SHORT_KERNEL - 1) + k
            conv = conv + cw_ref[k:k + 1, lo:lo + CHAN_CHUNK] * u_ref[first:first + t, lo:lo + CHAN_CHUNK]
        b_gate = _dot(h_ref[...], w_in_ref[:, cols(1)])
        z = _dot(h_ref[...], w_in_ref[:, cols(3)])
        y = (b_gate * conv * _silu(z)).astype(BF16)
        acc = acc + _dot(y, w_out_ref[lo:lo + CHAN_CHUNK, :])
    o_ref[0] = acc

    u_ref[0:SHORT_HALO, :] = u_ref[t:t + SHORT_HALO, :]


def _short_conv_layer(x, g, w_in, conv_w, w_out):
    b, s, d = x.shape
    t = SEQ_TILE
    return pl.pallas_call(
        _short_conv_kernel,
        grid=(b, s // t),
        in_specs=[
            pl.BlockSpec((1, t, d), lambda bi, si: (bi, si, 0)),
            _resident((1, d)),
            _resident((d, 4 * D_INNER)),
            _resident((SHORT_KERNEL, D_INNER)),
            _resident((D_INNER, d)),
        ],
        out_specs=pl.BlockSpec((1, t, d), lambda bi, si: (bi, si, 0)),
        out_shape=jax.ShapeDtypeStruct(x.shape, F32),
        scratch_shapes=[
            pltpu.VMEM((t, d), BF16),
            pltpu.VMEM((t + SHORT_HALO, D_INNER), F32),
        ],
        compiler_params=_params(2, n_inputs=5, fused_inputs=(2, 4)),
        name="short_conv_layer",
    )(x, g.reshape(1, d), w_in.astype(BF16), conv_w, w_out.astype(BF16))


def _fox_proj_kernel(x_ref, g_ref, wt_ref, wf_hi_ref, wf_lo_ref, fb_ref,
                     qg_ref, kg_ref, qx_ref, kx_ref, vt_ref, gate_ref, h_ref, carry_ref):
    si = pl.program_id(1)
    t = SEQ_TILE
    dh = FOX_HEAD_DIM

    @pl.when(si == 0)
    def _():
        carry_ref[...] = jnp.zeros_like(carry_ref)

    x = x_ref[0]
    hf = _rms_norm_rows(x, g_ref[...])
    h_ref[...] = hf.astype(BF16)

    h_hi = h_ref[...]
    h_lo = (hf - h_hi.astype(F32)).astype(BF16)
    f_logit = (_dot_nt(h_hi, wf_hi_ref[...]) + _dot_nt(h_hi, wf_lo_ref[...])
               + _dot_nt(h_lo, wf_hi_ref[...])) + fb_ref[...]
    log_f = jnp.minimum(f_logit, 0.0) - jnp.log1p(jnp.exp(-jnp.abs(f_logit)))
    rows = lax.broadcasted_iota(jnp.int32, (t, t), 0)
    cols = lax.broadcasted_iota(jnp.int32, (t, t), 1)
    tri = jnp.where(cols <= rows, 1.0, 0.0).astype(BF16)
    f_hi, f_mid, f_lo = _split3(log_f)
    c = (_dot(tri, f_hi) + _dot(tri, f_mid) + _dot(tri, f_lo)) + carry_ref[...]
    carry_ref[...] = c[t - 1:t, :]
    c_hi, c_mid, c_lo = _split3(c * LOG2_E)
    c_hi, c_mid, c_lo = c_hi.astype(F32), c_mid.astype(F32), c_lo.astype(F32)

    lane = lax.broadcasted_iota(jnp.int32, (t, dh), 1)
    scale = dh ** -0.5 * LOG2_E
    heads_per_chunk = CHAN_CHUNK // dh
    for c4 in range(D_INNER // CHAN_CHUNK):
        lo = c4 * CHAN_CHUNK
        q4 = _dot_nt(h_ref[...], wt_ref[lo:lo + CHAN_CHUNK, :])
        k4 = _dot_nt(h_ref[...], wt_ref[D_INNER + lo:D_INNER + lo + CHAN_CHUNK, :])
        for j in range(heads_per_chunk):
            hd = c4 * heads_per_chunk + j
            qh = _rms_norm_rows(q4[:, j * dh:(j + 1) * dh], qg_ref[...]) * scale
            kh = _rms_norm_rows(k4[:, j * dh:(j + 1) * dh], kg_ref[...])
            hi, mid, lw = c_hi[:, hd:hd + 1], c_mid[:, hd:hd + 1], c_lo[:, hd:hd + 1]
            q_ext = jnp.where(lane < 3, 1.0,
                              jnp.where(lane == 3, hi,
                                        jnp.where(lane == 4, mid,
                                                  jnp.where(lane == 5, lw, 0.0))))
            k_ext = jnp.where(lane == 0, -hi,
                              jnp.where(lane == 1, -mid,
                                        jnp.where(lane == 2, -lw,
                                                  jnp.where(lane < 6, 1.0, 0.0))))
            qx_ref[0, hd, :, 0:dh] = qh.astype(BF16)
            qx_ref[0, hd, :, dh:2 * dh] = q_ext.astype(BF16)
            kx_ref[0, hd, :, 0:dh] = kh.astype(BF16)
            kx_ref[0, hd, :, dh:2 * dh] = k_ext.astype(BF16)
        z = _dot_nt(h_ref[...], wt_ref[3 * D_INNER + lo:3 * D_INNER + lo + CHAN_CHUNK, :])
        gate_ref[0, :, lo:lo + CHAN_CHUNK] = _silu(z).astype(BF16)
        vt = _dot_nt(wt_ref[2 * D_INNER + lo:2 * D_INNER + lo + CHAN_CHUNK, :], h_ref[...])
        for j in range(heads_per_chunk):
            hd = c4 * heads_per_chunk + j
            vt_ref[0, hd, 0] = vt[j * dh:(j + 1) * dh, :].astype(BF16)


def _fox_proj(x, g, w_in, f_bias, q_norm, k_norm):
    b, s, d = x.shape
    t = SEQ_TILE
    h, dh = FOX_HEADS, FOX_HEAD_DIM
    e = D_INNER
    wt_f32 = w_in.T
    wt = wt_f32.astype(BF16)
    wf = wt_f32[4 * e:]
    wf_hi = wf.astype(BF16)
    wf_lo = (wf - wf_hi.astype(F32)).astype(BF16)
    return pl.pallas_call(
        _fox_proj_kernel,
        grid=(b, s // t),
        in_specs=[
            pl.BlockSpec((1, t, d), lambda bi, si: (bi, si, 0)),
            _resident((1, d)),
            _resident((4 * e + h, d)),
            _resident((h, d)), _resident((h, d)), _resident((1, h)),
            _resident((1, dh)), _resident((1, dh)),
        ],
        out_specs=[
            pl.BlockSpec((1, h, t, QK_WIDTH), lambda bi, si: (bi, 0, si, 0)),
            pl.BlockSpec((1, h, t, QK_WIDTH), lambda bi, si: (bi, 0, si, 0)),
            pl.BlockSpec((1, h, 1, dh, t), lambda bi, si: (bi, 0, si, 0, 0)),
            pl.BlockSpec((1, t, e), lambda bi, si: (bi, si, 0)),
        ],
        out_shape=[
            jax.ShapeDtypeStruct((b, h, s, QK_WIDTH), BF16),
            jax.ShapeDtypeStruct((b, h, s, QK_WIDTH), BF16),
            jax.ShapeDtypeStruct((b, h, s // t, dh, t), BF16),
            jax.ShapeDtypeStruct((b, s, e), BF16),
        ],
        scratch_shapes=[
            pltpu.VMEM((t, d), BF16),
            pltpu.VMEM((1, h), F32),
        ],
        compiler_params=_params(2, n_inputs=8, fused_inputs=(2,)),
        name="fox_proj",
    )(x, g.reshape(1, d), wt, wf_hi, wf_lo, f_bias.reshape(1, h),
      q_norm.reshape(1, dh), k_norm.reshape(1, dh))


def _fox_attn_kernel(qx_ref, kx_ref, vt_ref, gate_ref, y_ref,
                     s0_ref, s1_ref, p0_ref, p1_ref, a0_ref, a1_ref, m_ref, l_ref, acc_ref):
    qi = pl.program_id(2)
    kt = KV_TILE

    def scores(blk, s_ref, c0, c1):
        k0 = pl.multiple_of(blk * kt, kt)
        s_ref[:, c0:c1] = _dot_nt(kx_ref[0, 0, pl.ds(k0, kt), :], qx_ref[0, 0, c0:c1, :])

    def softmax(s_ref, p_ref, a_ref, c0, c1, masked):
        s = s_ref[:, c0:c1]
        if masked:
            kv_pos = lax.broadcasted_iota(jnp.int32, s.shape, 0)
            q_pos = lax.broadcasted_iota(jnp.int32, s.shape, 1)
            s = jnp.where(kv_pos <= q_pos, s, -jnp.inf)
        m_prev = m_ref[:, c0:c1]
        m_new = jnp.maximum(m_prev, jnp.max(s, axis=0, keepdims=True))
        p = jnp.exp2(s - m_new)
        alpha = jnp.exp2(m_prev - m_new)
        l_ref[:, c0:c1] = alpha * l_ref[:, c0:c1] + jnp.sum(p, axis=0, keepdims=True)
        m_ref[:, c0:c1] = m_new
        a_ref[:, c0:c1] = alpha
        p_ref[:, c0:c1] = p.astype(BF16)

    def accumulate(blk, p_ref, a_ref, c0, c1):
        acc_ref[:, c0:c1] = (a_ref[:, c0:c1] * acc_ref[:, c0:c1]
                             + _dot(vt_ref[0, 0, blk], p_ref[:, c0:c1]))

    m_ref[...] = jnp.full(m_ref.shape, -1e30, F32)
    l_ref[...] = jnp.zeros(l_ref.shape, F32)
    acc_ref[...] = jnp.zeros(acc_ref.shape, F32)
    p1_ref[...] = jnp.zeros(p1_ref.shape, BF16)
    a1_ref[...] = jnp.ones(a1_ref.shape, F32)

    full = (0, Q_TILE)
    scores(0, s0_ref, *full)

    def pair(t, carry):
        even = 2 * t
        scores(even + 1, s1_ref, *full)
        softmax(s0_ref, p0_ref, a0_ref, *full, masked=False)
        accumulate(jnp.maximum(even - 1, 0), p1_ref, a1_ref, *full)
        scores(even + 2, s0_ref, *full)
        softmax(s1_ref, p1_ref, a1_ref, *full, masked=False)
        accumulate(even, p0_ref, a0_ref, *full)
        return carry

    lax.fori_loop(0, qi, pair, 0)

    d0 = 2 * qi
    left, right = (0, kt), (kt, Q_TILE)
    scores(d0 + 1, s1_ref, *right)
    softmax(s0_ref, p0_ref, a0_ref, *left, masked=True)
    softmax(s0_ref, p0_ref, a0_ref, *right, masked=False)
    accumulate(jnp.maximum(d0 - 1, 0), p1_ref, a1_ref, *full)
    softmax(s1_ref, p1_ref, a1_ref, *right, masked=True)
    accumulate(d0, p0_ref, a0_ref, *full)
    accumulate(d0 + 1, p1_ref, a1_ref, *right)

    o_t = acc_ref[...] * (1.0 / l_ref[...])
    y_ref[0] = (o_t.T * gate_ref[0].astype(F32)).astype(BF16)


def _fox_attn_bounded_kernel(qx_ref, kx_ref, vt_ref, gate_ref, y_ref,
                             p0_ref, p1_ref, l_ref, acc_ref):
    qi = pl.program_id(2)
    kt = KV_TILE

    def probs(blk, p_ref, c0, c1, masked):
        k0 = pl.multiple_of(blk * kt, kt)
        s = _dot_nt(kx_ref[0, 0, pl.ds(k0, kt), :], qx_ref[0, 0, c0:c1, :])
        if masked:
            kv_pos = lax.broadcasted_iota(jnp.int32, s.shape, 0)
            q_pos = lax.broadcasted_iota(jnp.int32, s.shape, 1)
            s = jnp.where(kv_pos <= q_pos, s, -jnp.inf)
        p = jnp.exp2(s)
        l_ref[:, c0:c1] = l_ref[:, c0:c1] + jnp.sum(p, axis=0, keepdims=True)
        p_ref[:, c0:c1] = p.astype(BF16)

    def accumulate(blk, p_ref, c0, c1):
        acc_ref[:, c0:c1] = acc_ref[:, c0:c1] + _dot(vt_ref[0, 0, blk], p_ref[:, c0:c1])

    l_ref[...] = jnp.zeros(l_ref.shape, F32)
    acc_ref[...] = jnp.zeros(acc_ref.shape, F32)
    p1_ref[...] = jnp.zeros(p1_ref.shape, BF16)

    q_tile = p0_ref.shape[1]
    n_diag = q_tile // kt
    p_refs = (p0_ref, p1_ref)
    full = (0, q_tile)

    def pair(t, carry):
        even = 2 * t
        probs(even, p0_ref, *full, masked=False)
        accumulate(jnp.maximum(even - 1, 0), p1_ref, *full)
        probs(even + 1, p1_ref, *full, masked=False)
        accumulate(even, p0_ref, *full)
        return carry

    d0 = n_diag * qi
    lax.fori_loop(0, d0 // 2, pair, 0)

    def probs_diag(i):
        probs(d0 + i, p_refs[i % 2], i * kt, (i + 1) * kt, masked=True)
        if i + 1 < n_diag:
            probs(d0 + i, p_refs[i % 2], (i + 1) * kt, q_tile, masked=False)

    probs_diag(0)
    accumulate(jnp.maximum(d0 - 1, 0), p1_ref, *full)
    for i in range(1, n_diag):
        probs_diag(i)
        accumulate(d0 + i - 1, p_refs[(i - 1) % 2], (i - 1) * kt, q_tile)
    accumulate(d0 + n_diag - 1, p_refs[(n_diag - 1) % 2], (n_diag - 1) * kt, q_tile)

    o_t = acc_ref[...] * (1.0 / l_ref[...])
    y_ref[0] = (o_t.T * gate_ref[0].astype(F32)).astype(BF16)


def _fox_attention(qx, kx, vt, gate, *, bounded):
    b, h, s, _ = qx.shape
    dh = FOX_HEAD_DIM
    assert KV_TILE == SEQ_TILE and Q_TILE == 2 * KV_TILE and Q_TILE_BOUNDED % (2 * KV_TILE) == 0
    q_tile = Q_TILE_BOUNDED if bounded else Q_TILE
    probs = pltpu.VMEM((KV_TILE, q_tile), BF16)
    row = pltpu.VMEM((1, q_tile), F32)
    acc = pltpu.VMEM((dh, q_tile), F32)
    if bounded:
        body, scratch = _fox_attn_bounded_kernel, [probs, probs, row, acc]
    else:
        scores = pltpu.VMEM((KV_TILE, q_tile), F32)
        body, scratch = _fox_attn_kernel, [scores, scores, probs, probs, row, row, row, row, acc]
    return pl.pallas_call(
        body,
        grid=(b, h, s // q_tile),
        in_specs=[
            pl.BlockSpec((1, 1, q_tile, QK_WIDTH), lambda bi, hi, qi: (bi, hi, qi, 0)),
            pl.BlockSpec((1, 1, s, QK_WIDTH), lambda bi, hi, qi: (bi, hi, 0, 0)),
            pl.BlockSpec((1, 1, s // KV_TILE, dh, KV_TILE), lambda bi, hi, qi: (bi, hi, 0, 0, 0)),
            pl.BlockSpec((1, q_tile, dh), lambda bi, hi, qi: (bi, qi, hi)),
        ],
        out_specs=pl.BlockSpec((1, q_tile, dh), lambda bi, hi, qi: (bi, qi, hi)),
        out_shape=jax.ShapeDtypeStruct((b, s, h * dh), BF16),
        scratch_shapes=scratch,
        compiler_params=_params(3),
        name="fox_attention_bounded" if bounded else "fox_attention",
    )(qx, kx, vt, gate)


def _out_proj_kernel(x_ref, y_ref, w_ref, o_ref):
    o_ref[0] = x_ref[0] + _dot(y_ref[0], w_ref[...])


def _out_proj(x, y, w_out):
    b, s, d = x.shape
    t = SEQ_TILE
    e = y.shape[-1]
    return pl.pallas_call(
        _out_proj_kernel,
        grid=(b, s // t),
        in_specs=[
            pl.BlockSpec((1, t, d), lambda bi, si: (bi, si, 0)),
            pl.BlockSpec((1, t, e), lambda bi, si: (bi, si, 0)),
            _resident((e, d)),
        ],
        out_specs=pl.BlockSpec((1, t, d), lambda bi, si: (bi, si, 0)),
        out_shape=jax.ShapeDtypeStruct(x.shape, F32),
        compiler_params=_params(2, n_inputs=3, fused_inputs=(2,)),
        name="fox_out_proj",
    )(x, y, w_out.astype(BF16))


def _fox_layer(x, g, w_in, f_bias, q_norm, k_norm, w_out):
    qx, kx, vt, gate = _fox_proj(x, g, w_in, f_bias, q_norm, k_norm)
    qk_bound = (1.01 * FOX_HEAD_DIM ** 0.5 * LOG2_E) * jnp.max(jnp.abs(q_norm)) * jnp.max(jnp.abs(k_norm))
    y = lax.cond(qk_bound <= BOUNDED_LOGIT_LOG2,
                 functools.partial(_fox_attention, bounded=True),
                 functools.partial(_fox_attention, bounded=False),
                 qx, kx, vt, gate)
    return _out_proj(x, y, w_out)


def kernel(x, a_norm, a_w_in, a_conv_w, a_conv_b, a_ln_g, a_ln_b, a_w_out, b_norm, b_w_in, b_f_bias, b_q_norm, b_k_norm, b_w_out, c_norm, c_w_in, c_conv_w, c_w_out):
    depth = a_norm.shape[0] + b_norm.shape[0] + c_norm.shape[0]
    a_w_in_bf16, a_w_out_bf16 = a_w_in.astype(BF16), a_w_out.astype(BF16)
    for i in range(depth):
        kind, j = i % 3, i // 3
        if kind == 0:
            x = _conformer_layer(x, j, a_norm[j], a_w_in_bf16, a_conv_w[j], a_conv_b[j],
                                 a_ln_g[j], a_ln_b[j], a_w_out_bf16)
        elif kind == 1:
            x = _fox_layer(x, b_norm[j], b_w_in[j], b_f_bias[j], b_q_norm[j], b_k_norm[j],
                           b_w_out[j])
        else:
            x = _short_conv_layer(x, c_norm[j], c_w_in[j], c_conv_w[j], c_w_out[j])
    return x
```
